```python
import math
import jax
import jax.numpy as jnp
from jax import lax
import numpy as np

D_MODEL = 1024
BATCH = 2
SEQ = 16384
DEPTH = 2

GRID_W = 64
CTX_LEN = 256
Q_BLOCK = 128
ROPE_THETA = 10000.0
NORM_EPS = 1e-6
ALPHA = (2 * DEPTH) ** 0.25
BETA = (8 * DEPTH) ** -0.25
N_EVEN = (DEPTH + 1) // 2
N_ODD = DEPTH // 2
HEAD_DIM = 64
A_Q_HEADS = 12
A_KV_HEADS = 4
POOL_WINDOWS = (2, 4, 8, 16)
POOL_GROUP = 64
POOL_WIDTH = POOL_GROUP * len(POOL_WINDOWS)
C_HEADS = 8
C_Q_RANK = 384
C_KV_RANK = 256
C_NOPE = 64
C_ROPE = 32
C_V = 64
D_HEADS = 8
NA_WIN_H = 8
NA_WIN_W = 16
D_FF = 2816
N_EXPERTS = 8
TOP_K = 2
D_FF_EXPERT = 3584
MOE_BLOCK = 256

A_Q_W = A_Q_HEADS * HEAD_DIM
A_KV_W = A_KV_HEADS * HEAD_DIM
EVEN_IN = A_Q_W + 2 * A_KV_W + POOL_WIDTH
EVEN_MIX = A_Q_W + POOL_WIDTH
EVEN_SPLITS = (A_Q_W, A_Q_W + A_KV_W, A_Q_W + 2 * A_KV_W)
NA_W = D_HEADS * HEAD_DIM
ODD_IN = C_Q_RANK + C_KV_RANK + C_ROPE + 3 * NA_W
ODD_MIX = C_HEADS * C_V + NA_W
ODD_SPLITS = (C_Q_RANK, C_Q_RANK + C_KV_RANK, C_Q_RANK + C_KV_RANK + C_ROPE,
              C_Q_RANK + C_KV_RANK + C_ROPE + NA_W, C_Q_RANK + C_KV_RANK + C_ROPE + 2 * NA_W)

kernel_name = 'hybrid_flow_backbone'


def rms_norm(x, g):
    xf = x.astype(jnp.float32)
    y = xf * lax.rsqrt(jnp.mean(xf * xf, axis=-1, keepdims=True) + NORM_EPS)
    return (y * g.astype(jnp.float32)).astype(x.dtype)


def layer_norm(x, g, b):
    xf = x.astype(jnp.float32)
    mu = jnp.mean(xf, axis=-1, keepdims=True)
    var = jnp.mean(jnp.square(xf - mu), axis=-1, keepdims=True)
    y = (xf - mu) * lax.rsqrt(var + NORM_EPS)
    return (y * g.astype(jnp.float32) + b.astype(jnp.float32)).astype(x.dtype)


def modulate(x, shift, scale):
    return x * (1 + scale) + shift


def grid_positions(length):
    t = jnp.arange(length, dtype=jnp.int32)
    return (t // GRID_W).astype(jnp.float32), (t % GRID_W).astype(jnp.float32)


def rope_axis(x, pos):
    half = x.shape[-1] // 2
    inv = ROPE_THETA ** (-jnp.arange(half, dtype=jnp.float32) / half)
    ang = pos[:, None] * inv[None, :]
    cos = jnp.cos(ang)[None, :, None, :]
    sin = jnp.sin(ang)[None, :, None, :]
    x1, x2 = x[..., :half], x[..., half:]
    return jnp.concatenate([x1 * cos - x2 * sin, x1 * sin + x2 * cos], axis=-1).astype(x.dtype)


def rope_2d(x, row, col):
    n = x.shape[-1] // 2
    return jnp.concatenate([rope_axis(x[..., :n], row), rope_axis(x[..., n:], col)], axis=-1)


def block_attention(q, k, v):
    b, l, hq, dk = q.shape
    hkv, dv = k.shape[2], v.shape[-1]
    g = hq // hkv
    nb = l // Q_BLOCK
    scale = dk ** -0.5
    qb = q.reshape(b, nb, Q_BLOCK, hkv, g, dk).transpose(1, 0, 2, 3, 4, 5)

    def one(q_blk):
        s = jnp.einsum('bqhgd,bkhd->bhgqk', q_blk, k, preferred_element_type=jnp.float32) * scale
        p = jax.nn.softmax(s, axis=-1).astype(v.dtype)
        return jnp.einsum('bhgqk,bkhd->bqhgd', p, v)

    o = lax.map(one, qb)
    return o.transpose(1, 0, 2, 3, 4, 5).reshape(b, l, hq * dv)


def multiscale_pool(x, w_pool, pool_scale):
    b, l, _ = x.shape
    ng = len(POOL_WINDOWS)
    xg = x.reshape(b, l, ng, POOL_GROUP).astype(jnp.float32)
    csum = jnp.concatenate([jnp.zeros((b, 1, ng, POOL_GROUP), jnp.float32), jnp.cumsum(xg, axis=1)], axis=1)
    win = jnp.array(POOL_WINDOWS, dtype=jnp.int32)
    lo = win // 2
    hi = win - lo
    t = jnp.arange(l, dtype=jnp.int32)[:, None]
    start = jnp.clip(t - lo[None, :], 0, l)
    end = jnp.clip(t + hi[None, :], 0, l)
    gidx = jnp.arange(ng)[None, :]
    mean = (csum[:, end, gidx] - csum[:, start, gidx]) / (end - start).astype(jnp.float32)[None, :, :, None]
    y = (mean - xg).astype(x.dtype)
    y = jnp.einsum('blgc,gcd->blgd', y, w_pool).reshape(b, l, POOL_WIDTH)
    return y * pool_scale


def neighborhood_attention(q, k, v, k_ctx, v_ctx, rel_bias):
    b, l, h, d = q.shape
    rows = l // GRID_W
    kh = min(NA_WIN_H, rows)
    kw = NA_WIN_W
    scale = d ** -0.5
    qg = q.reshape(b, rows, GRID_W, h, d).transpose(1, 0, 2, 3, 4)
    kg = k.reshape(b, rows, GRID_W, h, d)
    vg = v.reshape(b, rows, GRID_W, h, d)
    j = jnp.arange(GRID_W, dtype=jnp.int32)
    col_start = jnp.clip(j - kw // 2, 0, GRID_W - kw)
    col_valid = (j[None, :] >= col_start[:, None]) & (j[None, :] < col_start[:, None] + kw)
    col_idx = jnp.clip(j[None, :] - j[:, None] + NA_WIN_W - 1, 0, 2 * NA_WIN_W - 2)

    def one_row(args):
        q_row, r = args
        rs = jnp.clip(r - kh // 2, 0, rows - kh)
        k_band = lax.dynamic_slice_in_dim(kg, rs, kh, axis=1)
        v_band = lax.dynamic_slice_in_dim(vg, rs, kh, axis=1)
        row_idx = rs + jnp.arange(kh, dtype=jnp.int32) - r + NA_WIN_H - 1
        bias = rel_bias[:, row_idx[:, None, None], col_idx[None, :, :]]
        s = jnp.einsum('bqhd,bikhd->bhqik', q_row, k_band, preferred_element_type=jnp.float32) * scale
        s = s + bias.transpose(0, 2, 1, 3)[None].astype(jnp.float32)
        s = jnp.where(col_valid[None, None, :, None, :], s, -jnp.inf)
        s_ctx = jnp.einsum('bqhd,bkhd->bhqk', q_row, k_ctx, preferred_element_type=jnp.float32) * scale
        p = jax.nn.softmax(jnp.concatenate([s.reshape(b, h, GRID_W, kh * GRID_W), s_ctx], axis=-1), axis=-1)
        p = p.astype(v.dtype)
        p_win = p[..., :kh * GRID_W].reshape(b, h, GRID_W, kh, GRID_W)
        p_ctx = p[..., kh * GRID_W:]
        return (jnp.einsum('bhqik,bikhd->bqhd', p_win, v_band)
                + jnp.einsum('bhqk,bkhd->bqhd', p_ctx, v_ctx))

    o = lax.map(one_row, (qg, jnp.arange(rows, dtype=jnp.int32)))
    return o.transpose(1, 0, 2, 3, 4).reshape(b, l, h * d)


def swiglu(u, w_gate, w_up, w_down):
    return (jax.nn.silu(u @ w_gate) * (u @ w_up)) @ w_down


def moe_swiglu(u, w_router, b_router, w_gate, w_up, w_down):
    shp = u.shape
    x = u.reshape(-1, shp[-1])
    t = x.shape[0]
    logits = jnp.dot(x, w_router, preferred_element_type=jnp.float32) + b_router.astype(jnp.float32)
    top_val, top_idx = lax.top_k(logits, TOP_K)
    gates = jax.nn.softmax(top_val, axis=-1)
    n_assign = t * TOP_K
    e_flat = top_idx.reshape(-1)
    order = jnp.argsort(e_flat)
    e_sorted = e_flat[order]
    tok_sorted = order // TOP_K
    g_sorted = gates.reshape(-1)[order]
    counts = jnp.bincount(e_flat, length=N_EXPERTS)
    padded = (counts + MOE_BLOCK - 1) // MOE_BLOCK * MOE_BLOCK
    start = jnp.cumsum(counts) - counts
    pad_end = jnp.cumsum(padded)
    pad_start = pad_end - padded
    dest = pad_start[e_sorted] + jnp.arange(n_assign, dtype=jnp.int32) - start[e_sorted]
    n_blocks = -(-n_assign // MOE_BLOCK) + N_EXPERTS
    x_pad = jnp.zeros((n_blocks * MOE_BLOCK, shp[-1]), x.dtype).at[dest].set(x[tok_sorted])
    blk_expert = jnp.minimum(
        jnp.searchsorted(pad_end, jnp.arange(n_blocks, dtype=jnp.int32) * MOE_BLOCK, side='right'),
        N_EXPERTS - 1)

    def one_block(args):
        xb, e = args
        return swiglu(xb, w_gate[e], w_up[e], w_down[e])

    y_pad = lax.map(one_block, (x_pad.reshape(n_blocks, MOE_BLOCK, shp[-1]), blk_expert))
    y_pad = y_pad.reshape(n_blocks * MOE_BLOCK, shp[-1])
    y = jnp.zeros_like(x).at[tok_sorted].add(y_pad[dest] * g_sorted[:, None].astype(x.dtype))
    return y.reshape(shp)


def even_mixer(u, u_ctx, w_in, w_out, q_gain, k_gain, w_pool, pool_scale, row, col, need_ctx):
    def project(h):
        b, l, _ = h.shape
        q, k, v, p = jnp.split(h, EVEN_SPLITS, axis=-1)
        q = rms_norm(q.reshape(b, l, A_Q_HEADS, HEAD_DIM), q_gain)
        k = rms_norm(k.reshape(b, l, A_KV_HEADS, HEAD_DIM), k_gain)
        return q, k, v.reshape(b, l, A_KV_HEADS, HEAD_DIM), p

    q, k, v, p = project(u @ w_in)
    qc, kc, vc, pc = project(u_ctx @ w_in)
    q = rope_2d(q, row, col)
    k = rope_2d(k, row, col)
    attn = block_attention(q, jnp.concatenate([kc, k], axis=1), jnp.concatenate([vc, v], axis=1))
    y = jnp.concatenate([attn, multiscale_pool(p, w_pool, pool_scale)], axis=-1) @ w_out
    y_ctx = None
    if need_ctx:
        attn_c = block_attention(qc, kc, vc)
        y_ctx = jnp.concatenate([attn_c, multiscale_pool(pc, w_pool, pool_scale)], axis=-1) @ w_out
    return y, y_ctx


def odd_mixer(u, u_ctx, w_in, w_out, q_lat_gain, kv_lat_gain, w_q_up, w_kv_up, na_bias, row, col, need_ctx):
    def project(h, rotate):
        b, l, _ = h.shape
        cq, ckv, kr, nq, nk, nv = jnp.split(h, ODD_SPLITS, axis=-1)
        q = (rms_norm(cq, q_lat_gain) @ w_q_up).reshape(b, l, C_HEADS, C_NOPE + C_ROPE)
        kv = (rms_norm(ckv, kv_lat_gain) @ w_kv_up).reshape(b, l, C_HEADS, C_NOPE + C_V)
        q_nope, q_rope = q[..., :C_NOPE], q[..., C_NOPE:]
        k_nope, v = kv[..., :C_NOPE], kv[..., C_NOPE:]
        kr = kr.reshape(b, l, 1, C_ROPE)
        if rotate:
            q_rope = rope_2d(q_rope, row, col)
            kr = rope_2d(kr, row, col)
        q = jnp.concatenate([q_nope, q_rope], axis=-1)
        k = jnp.concatenate([k_nope, jnp.broadcast_to(kr, (b, l, C_HEADS, C_ROPE))], axis=-1)
        shape_na = (b, l, D_HEADS, HEAD_DIM)
        return q, k, v, nq.reshape(shape_na), nk.reshape(shape_na), nv.reshape(shape_na)

    q, k, v, nq, nk, nv = project(u @ w_in, True)
    qc, kc, vc, nqc, nkc, nvc = project(u_ctx @ w_in, False)
    mla = block_attention(q, jnp.concatenate([kc, k], axis=1), jnp.concatenate([vc, v], axis=1))
    na = neighborhood_attention(nq, nk, nv, nkc, nvc, na_bias)
    y = jnp.concatenate([mla, na], axis=-1) @ w_out
    y_ctx = None
    if need_ctx:
        y_ctx = jnp.concatenate([block_attention(qc, kc, vc), block_attention(nqc, nkc, nvc)], axis=-1) @ w_out
    return y, y_ctx


def setup_inputs(seed: int = 0) -> dict:
    key = jax.random.key(seed)
    ks = iter(jax.random.split(key, 48))
    f32 = jnp.float32

    def nrm(shape, fan_in, s=1.0):
        return jax.random.normal(next(ks), shape, f32) * (s * fan_in ** -0.5)

    def gain(shape):
        return 1.0 + 0.05 * jax.random.normal(next(ks), shape, f32)

    def small(shape, s):
        return s * jax.random.normal(next(ks), shape, f32)

    d = D_MODEL
    return {
        'x': jax.random.normal(next(ks), (BATCH, SEQ, d), f32),
        'c': jax.random.normal(next(ks), (BATCH, d), f32),
        'ctx': jax.random.normal(next(ks), (BATCH, CTX_LEN, d), f32),
        'c_ctx': jax.random.normal(next(ks), (d,), f32),
        'w_ada': nrm((DEPTH, d, 6 * d), d, 0.5),
        'b_ada': small((DEPTH, 6 * d), 0.02),
        'ln1_g': gain((DEPTH, d)),
        'ln1_b': small((DEPTH, d), 0.02),
        'ln2_g': gain((DEPTH, d)),
        'ln2_b': small((DEPTH, d), 0.02),
        'ev_w_in': nrm((N_EVEN, d, EVEN_IN), d),
        'ev_w_out': nrm((N_EVEN, EVEN_MIX, d), EVEN_MIX, BETA),
        'ev_q_gain': gain((N_EVEN, HEAD_DIM)),
        'ev_k_gain': gain((N_EVEN, HEAD_DIM)),
        'ev_w_pool': nrm((N_EVEN, len(POOL_WINDOWS), POOL_GROUP, POOL_GROUP), POOL_GROUP),
        'ev_pool_scale': gain((N_EVEN, POOL_WIDTH)),
        'ev_w_gate': nrm((N_EVEN, d, D_FF), d),
        'ev_w_up': nrm((N_EVEN, d, D_FF), d),
        'ev_w_down': nrm((N_EVEN, D_FF, d), D_FF, BETA),
        'od_w_in': nrm((N_ODD, d, ODD_IN), d),
        'od_w_out': nrm((N_ODD, ODD_MIX, d), ODD_MIX, BETA),
        'od_q_lat_gain': gain((N_ODD, C_Q_RANK)),
        'od_kv_lat_gain': gain((N_ODD, C_KV_RANK)),
        'od_w_q_up': nrm((N_ODD, C_Q_RANK, C_HEADS * (C_NOPE + C_ROPE)), C_Q_RANK),
        'od_w_kv_up': nrm((N_ODD, C_KV_RANK, C_HEADS * (C_NOPE + C_V)), C_KV_RANK),
        'od_na_bias': small((N_ODD, D_HEADS, 2 * NA_WIN_H - 1, 2 * NA_WIN_W - 1), 0.2),
        'od_w_router': nrm((N_ODD, d, N_EXPERTS), d),
        'od_b_router': small((N_ODD, N_EXPERTS), 0.01),
        'od_w_gate': nrm((N_ODD, N_EXPERTS, d, D_FF_EXPERT), d),
        'od_w_up': nrm((N_ODD, N_EXPERTS, d, D_FF_EXPERT), d),
        'od_w_down': nrm((N_ODD, N_EXPERTS, D_FF_EXPERT, d), D_FF_EXPERT, BETA),
    }


def reference(x, c, ctx, c_ctx, w_ada, b_ada, ln1_g, ln1_b, ln2_g, ln2_b,
              ev_w_in, ev_w_out, ev_q_gain, ev_k_gain, ev_w_pool, ev_pool_scale,
              ev_w_gate, ev_w_up, ev_w_down,
              od_w_in, od_w_out, od_q_lat_gain, od_kv_lat_gain, od_w_q_up, od_w_kv_up, od_na_bias,
              od_w_router, od_b_router, od_w_gate, od_w_up, od_w_down):
    row, col = grid_positions(x.shape[1])
    silu_c = jax.nn.silu(c)
    silu_cc = jax.nn.silu(c_ctx)
    x_ctx = ctx
    for l in range(DEPTH):
        need_ctx = l < DEPTH - 1
        i = l // 2
        mod = jnp.split((silu_c @ w_ada[l] + b_ada[l])[:, None, :], 6, axis=-1)
        mod_c = jnp.split((silu_cc @ w_ada[l] + b_ada[l])[None, None, :], 6, axis=-1)
        u = modulate(x, mod[0], mod[1])
        u_c = modulate(x_ctx, mod_c[0], mod_c[1])
        if l % 2 == 0:
            y, y_c = even_mixer(u, u_c, ev_w_in[i], ev_w_out[i], ev_q_gain[i], ev_k_gain[i],
                                ev_w_pool[i], ev_pool_scale[i], row, col, need_ctx)
            ffn = lambda h, i=i: swiglu(h, ev_w_gate[i], ev_w_up[i], ev_w_down[i])
        else:
            y, y_c = odd_mixer(u, u_c, od_w_in[i], od_w_out[i], od_q_lat_gain[i], od_kv_lat_gain[i],
                               od_w_q_up[i], od_w_kv_up[i], od_na_bias[i], row, col, need_ctx)
            ffn = lambda h, i=i: moe_swiglu(h, od_w_router[i], od_b_router[i],
                                            od_w_gate[i], od_w_up[i], od_w_down[i])
        x = layer_norm(ALPHA * x + mod[2] * y, ln1_g[l], ln1_b[l])
        x = layer_norm(ALPHA * x + mod[5] * ffn(modulate(x, mod[3], mod[4])), ln2_g[l], ln2_b[l])
        if need_ctx:
            x_ctx = layer_norm(ALPHA * x_ctx + mod_c[2] * y_c, ln1_g[l], ln1_b[l])
            x_ctx = layer_norm(ALPHA * x_ctx + mod_c[5] * ffn(modulate(x_ctx, mod_c[3], mod_c[4])),
                               ln2_g[l], ln2_b[l])
    return x
```

```python
import functools
import math

import numpy as np
import jax
import jax.numpy as jnp
from jax import lax
from jax.experimental import pallas as pl
from jax.experimental.pallas import tpu as pltpu

F32 = jnp.float32
BF16 = jnp.bfloat16

D_MODEL = 1024
DEPTH = 2
GRID_W = 64
ROPE_THETA = 10000.0
NORM_EPS = 1e-6
ALPHA = (2 * DEPTH) ** 0.25
HEAD_DIM = 64
A_Q_HEADS = 12
A_KV_HEADS = 4
A_GROUP = A_Q_HEADS // A_KV_HEADS
POOL_WINDOWS = (2, 4, 8, 16)
POOL_GROUP = 64
POOL_WIDTH = POOL_GROUP * len(POOL_WINDOWS)
C_HEADS = 8
C_Q_RANK = 384
C_KV_RANK = 256
C_NOPE = 64
C_ROPE = 32
C_V = 64
D_HEADS = 8
NA_WIN_H = 8
NA_WIN_W = 16
D_FF = 2816
N_EXPERTS = 8
D_FF_EXPERT = 3584
A_Q_W = A_Q_HEADS * HEAD_DIM
A_KV_W = A_KV_HEADS * HEAD_DIM
NA_W = D_HEADS * HEAD_DIM

LANES = 128
TM = 256
VMEM_CAP = 56 * 1024 * 1024
MOE_BM = 512
MOE_TF = 512
NEG = -1e30
LOG2E = 1.4426950408889634


def _cparams(sem, vmem_bytes):
    return pltpu.CompilerParams(dimension_semantics=sem,
                                vmem_limit_bytes=int(min(VMEM_CAP, max(vmem_bytes, 16 * 1024 * 1024))))


def _dot(a, b):
    return jnp.dot(a, b, preferred_element_type=F32)


def _dot_nt(a, b):
    return lax.dot_general(a, b, (((1,), (1,)), ((), ())), preferred_element_type=F32)


def _layer_norm(z, g, b):
    mu = jnp.mean(z, axis=-1, keepdims=True)
    zc = z - mu
    var = jnp.mean(zc * zc, axis=-1, keepdims=True)
    return zc * lax.rsqrt(var + NORM_EPS) * g + b


def _ada_kernel(c_ref, w_ref, b_ref, o_ref):
    c = c_ref[...]
    s = c * jax.nn.sigmoid(c)
    o_ref[...] = _dot(s.astype(BF16), w_ref[...].astype(BF16)) + b_ref[...]


def _ada_table(cin, w_ada, b_ada):
    depth, d, n = w_ada.shape
    return pl.pallas_call(
        _ada_kernel,
        grid=(depth, n // d),
        in_specs=[pl.BlockSpec((8, d), lambda l, j: (0, 0)),
                  pl.BlockSpec((None, d, d), lambda l, j: (l, 0, j)),
                  pl.BlockSpec((None, 1, d), lambda l, j: (l, 0, j))],
        out_specs=pl.BlockSpec((None, 8, d), lambda l, j: (l, 0, j)),
        out_shape=jax.ShapeDtypeStruct((depth, 8, n), F32),
        compiler_params=_cparams(("arbitrary", "arbitrary"), 24 << 20),
        name="ada_table",
    )(cin, w_ada, b_ada.reshape(depth, 1, n))


def _mod_spec(layer, k, nargs):
    if nargs == 1:
        return pl.BlockSpec((None, 8, D_MODEL), lambda j: (layer, 0, k))
    return pl.BlockSpec((None, 8, D_MODEL), lambda *_: (layer, 0, k))


def _mod_row_all(j, nbpb):
    return jnp.where(j % nbpb == 0, 0, 1 + j // nbpb)


def _rope_axis_tables(rope_dims, seq, ctx):
    n = rope_dims // 2
    half = n // 2
    t = jnp.arange(seq, dtype=jnp.int32)
    row = (t // GRID_W).astype(F32)
    col = (t % GRID_W).astype(F32)
    inv = ROPE_THETA ** (-jnp.arange(half, dtype=F32) / half)
    cols_c, cols_s = [], []
    for pos in (row, col):
        ang = pos[:, None] * inv[None, :]
        c, s = jnp.cos(ang), jnp.sin(ang)
        cols_c += [c, c]
        cols_s += [-s, s]
    cos = jnp.concatenate(cols_c, axis=-1)
    sin = jnp.concatenate(cols_s, axis=-1)
    cos = jnp.concatenate([jnp.ones((ctx, rope_dims), F32), cos], axis=0)
    sin = jnp.concatenate([jnp.zeros((ctx, rope_dims), F32), sin], axis=0)
    d = np.arange(rope_dims)
    partner = (d // n) * n + ((d % n) + half) % n
    return cos, sin, partner


def _even_inproj_kernel(x_ref, sh_ref, sc_ref, w_ref, cos_ref, sin_ref, gv_ref, ones_ref,
                        qt_ref, k_ref, vt_ref, p_ref, *, nbpb):
    j = pl.program_id(0)
    r = _mod_row_all(j, nbpb)
    shift = sh_ref[pl.ds(r, 1), :]
    scale = sc_ref[pl.ds(r, 1), :]
    u = x_ref[...] * (1.0 + scale) + shift
    acc = _dot(u.astype(BF16), w_ref[...])
    cos = cos_ref[...]
    sin = sin_ref[...]
    ones_bd = ones_ref[...]

    def norm_rope(a, ap, g, gp):
        ssq = _dot((a * a).astype(BF16), ones_bd)
        rinv = lax.rsqrt(ssq * (1.0 / HEAD_DIM) + NORM_EPS)
        return (a * g * cos + ap * gp * sin) * rinv

    gq, gqp, gk, gkp = gv_ref[0:1, :], gv_ref[1:2, :], gv_ref[2:3, :], gv_ref[3:4, :]
    p_off = A_Q_W + 2 * A_KV_W + POOL_WIDTH
    zeros_half = jnp.zeros((HEAD_DIM, TM), BF16)
    for c in range(A_Q_W // LANES):
        a = acc[:, c * LANES:(c + 1) * LANES]
        ap = acc[:, p_off + c * LANES:p_off + (c + 1) * LANES]
        rot = norm_rope(a, ap, gq, gqp) * (HEAD_DIM ** -0.5 * LOG2E)
        rot_t = rot.T
        for hl in range(2):
            head = 2 * c + hl
            slot = (head // A_GROUP) % 2
            base = head * LANES
            qt_ref[base + slot * HEAD_DIM:base + (slot + 1) * HEAD_DIM, :] = (
                rot_t[hl * HEAD_DIM:(hl + 1) * HEAD_DIM, :].astype(BF16))
            qt_ref[base + (1 - slot) * HEAD_DIM:base + (2 - slot) * HEAD_DIM, :] = zeros_half
    for c in range(A_KV_W // LANES):
        a = acc[:, A_Q_W + c * LANES:A_Q_W + (c + 1) * LANES]
        ap = acc[:, p_off + A_Q_W + c * LANES:p_off + A_Q_W + (c + 1) * LANES]
        k_ref[:, c * LANES:(c + 1) * LANES] = norm_rope(a, ap, gk, gkp).astype(BF16)
    v = acc[:, A_Q_W + A_KV_W:A_Q_W + 2 * A_KV_W]
    vt_ref[...] = v.T.astype(BF16)
    p_ref[...] = acc[:, A_Q_W + 2 * A_KV_W:A_Q_W + 2 * A_KV_W + POOL_WIDTH]


def _even_inproj(xa, modt, w_ext, cos2, sin2, gvec, ones_bd, *, batch, nt):
    nbpb = nt // TM
    nblk = batch * nbpb
    n_ext = w_ext.shape[1]
    kern = functools.partial(_even_inproj_kernel, nbpb=nbpb)
    return pl.pallas_call(
        kern,
        grid=(nblk,),
        in_specs=[pl.BlockSpec((TM, D_MODEL), lambda j: (j, 0)),
                  _mod_spec(0, 0, 1), _mod_spec(0, 1, 1),
                  pl.BlockSpec((D_MODEL, n_ext), lambda j: (0, 0)),
                  pl.BlockSpec((TM, LANES), lambda j: (j % nbpb, 0)),
                  pl.BlockSpec((TM, LANES), lambda j: (j % nbpb, 0)),
                  pl.BlockSpec((8, LANES), lambda j: (0, 0)),
                  pl.BlockSpec((LANES, LANES), lambda j: (0, 0))],
        out_specs=[pl.BlockSpec((None, A_Q_HEADS * LANES, TM), lambda j: (j // nbpb, 0, j % nbpb)),
                   pl.BlockSpec((TM, A_KV_W), lambda j: (j, 0)),
                   pl.BlockSpec((None, None, A_KV_W, TM), lambda j: (j // nbpb, j % nbpb, 0, 0)),
                   pl.BlockSpec((TM, POOL_WIDTH), lambda j: (j, 0))],
        out_shape=[jax.ShapeDtypeStruct((batch, A_Q_HEADS * LANES, nt), BF16),
                   jax.ShapeDtypeStruct((batch * nt, A_KV_W), BF16),
                   jax.ShapeDtypeStruct((batch, nbpb, A_KV_W, TM), BF16),
                   jax.ShapeDtypeStruct((batch * nt, POOL_WIDTH), F32)],
        compiler_params=_cparams(("arbitrary",), 40 << 20),
        name="even_inproj",
    )(xa, modt, modt, w_ext, cos2, sin2, gvec, ones_bd)


def _flash_kernel(q_ref, k_ref, v_ref, o_ref, acc_ref, *, nq, k_off, v_off, ctx_queries):
    nch = v_ref.shape[0]
    if ctx_queries:
        n_ch = jnp.where(pl.program_id(2) == 0, 1, nch)
    else:
        n_ch = nch
    acc_ref[...] = jnp.zeros_like(acc_ref)

    def body(c, carry):
        ms, ls = carry
        start = pl.multiple_of(c * TM, TM)
        new_m, new_l = [], []
        for j in range(nq):
            kc = k_ref[pl.ds(start, TM), k_off[j]:k_off[j] + LANES]
            s = _dot(kc, q_ref[j * LANES:(j + 1) * LANES, :])
            m_new = jnp.maximum(ms[j], jnp.max(s, axis=0, keepdims=True))
            alpha = jnp.exp2(ms[j] - m_new)
            p = jnp.exp2(s - m_new)
            new_l.append(alpha * ls[j] + jnp.sum(p, axis=0, keepdims=True))
            new_m.append(m_new)
            pv = _dot(v_ref[c, v_off[j]:v_off[j] + HEAD_DIM, :], p.astype(BF16))
            rows = slice(j * HEAD_DIM, (j + 1) * HEAD_DIM)
            acc_ref[rows, :] = alpha * acc_ref[rows, :] + pv
        return tuple(new_m), tuple(new_l)

    m0 = tuple(jnp.full((1, TM), NEG, F32) for _ in range(nq))
    l0 = tuple(jnp.zeros((1, TM), F32) for _ in range(nq))
    _, ls = lax.fori_loop(0, n_ch, body, (m0, l0))
    for j in range(nq):
        rows = slice(j * HEAD_DIM, (j + 1) * HEAD_DIM)
        acc_ref[rows, :] = acc_ref[rows, :] / ls[j]
    o_ref[...] = acc_ref[...].T.astype(BF16)


def _flash(qt, k, vt, *, batch, nt, units, nq, kw, k_off, v_off, ctx_queries, name):
    nbpb = nt // TM
    nqb = nbpb if ctx_queries else nbpb - 1
    qoff = 0 if ctx_queries else 1
    nv = vt.shape[2] // units
    kern = functools.partial(_flash_kernel, nq=nq, k_off=k_off, v_off=v_off, ctx_queries=ctx_queries)
    vmem = 2 * (nt * kw * 2 + nt * nv * 2) + 4 * nq * LANES * TM * 2 + 3 * nq * HEAD_DIM * TM * 4 + (8 << 20)
    return pl.pallas_call(
        kern,
        grid=(batch, units, nqb),
        in_specs=[pl.BlockSpec((None, nq * LANES, TM), lambda b, u, i: (b, u, i + qoff)),
                  pl.BlockSpec((None, nt, kw), lambda b, u, i: (b, 0, u)),
                  pl.BlockSpec((None, nbpb, nv, TM), lambda b, u, i: (b, 0, u, 0))],
        out_specs=pl.BlockSpec((None, TM, nq * HEAD_DIM), lambda b, u, i: (b, i, u)),
        out_shape=jax.ShapeDtypeStruct((batch, nqb * TM, units * nq * HEAD_DIM), BF16),
        scratch_shapes=[pltpu.VMEM((nq * HEAD_DIM, TM), F32)],
        compiler_params=_cparams(("arbitrary", "arbitrary", "arbitrary"), vmem),
        name=name,
    )(qt, k, vt)


def _pool_kernel(prev_ref, cur_ref, next_ref, w_ref, ps_ref, o_ref, *, nbpb, nt):
    jj = pl.program_id(0) % nbpb
    base = jj * TM
    seg_lo = jnp.where(jj == 0, 0, TM)
    seg_hi = jnp.where(jj == 0, TM, nt)
    cur = cur_ref[...]
    xcat = jnp.concatenate([prev_ref[...], cur, next_ref[...]], axis=0)
    x_hi = xcat.astype(BF16)
    x_lo = (xcat - x_hi.astype(F32)).astype(BF16)
    t = base + lax.broadcasted_iota(jnp.int32, (TM, 3 * TM), 0)
    s = base - TM + lax.broadcasted_iota(jnp.int32, (TM, 3 * TM), 1)
    t1 = base + lax.broadcasted_iota(jnp.int32, (TM, 1), 0)
    lane_grp = lax.broadcasted_iota(jnp.int32, (TM, POOL_WIDTH), 1) // POOL_GROUP
    mean = jnp.zeros((TM, POOL_WIDTH), F32)
    for g, win in enumerate(POOL_WINDOWS):
        lo = win // 2
        hi = win - lo
        valid = ((s >= jnp.maximum(t - lo, seg_lo)) & (s < jnp.minimum(t + hi, seg_hi)))
        vb = valid.astype(F32).astype(BF16)
        wsum = _dot(vb, x_hi) + _dot(vb, x_lo)
        cnt = (jnp.minimum(t1 + hi, seg_hi) - jnp.maximum(t1 - lo, seg_lo)).astype(F32)
        mean = jnp.where(lane_grp == g, wsum / cnt, mean)
    y = (mean - cur).astype(BF16)
    o_ref[...] = (_dot(y, w_ref[...]) * ps_ref[...]).astype(BF16)


def _pool(p, w_bd, ps, *, batch, nt):
    nbpb = nt // TM
    nblk = batch * nbpb
    kern = functools.partial(_pool_kernel, nbpb=nbpb, nt=nt)

    def prev_map(j):
        return (jnp.maximum(j - 1, 0), 0)

    def next_map(j):
        return (jnp.minimum(j + 1, nblk - 1), 0)

    return pl.pallas_call(
        kern,
        grid=(nblk,),
        in_specs=[pl.BlockSpec((TM, POOL_WIDTH), prev_map),
                  pl.BlockSpec((TM, POOL_WIDTH), lambda j: (j, 0)),
                  pl.BlockSpec((TM, POOL_WIDTH), next_map),
                  pl.BlockSpec((POOL_WIDTH, POOL_WIDTH), lambda j: (0, 0)),
                  pl.BlockSpec((1, POOL_WIDTH), lambda j: (0, 0))],
        out_specs=pl.BlockSpec((TM, POOL_WIDTH), lambda j: (j, 0)),
        out_shape=jax.ShapeDtypeStruct((batch * nt, POOL_WIDTH), BF16),
        compiler_params=_cparams(("arbitrary",), 24 << 20),
        name="pool_mixer",
    )(p, p, p, w_bd, ps)


def _proj_ln_kernel(a_ref, b_ref, wa_ref, wb_ref, x_ref, gate_ref, g_ref, beta_ref, o_ref, *, row_fn):
    r = row_fn(pl.program_id(0))
    gate = gate_ref[pl.ds(r, 1), :]
    y = _dot(a_ref[...], wa_ref[...]) + _dot(b_ref[...], wb_ref[...])
    o_ref[...] = _layer_norm(ALPHA * x_ref[...] + gate * y, g_ref[...], beta_ref[...])


def _proj_ln(a, b, wa, wb, x, modt, g, beta, *, layer, nblk, x_map, row_fn, name):
    ka, kb = a.shape[1], b.shape[1]
    kern = functools.partial(_proj_ln_kernel, row_fn=row_fn)
    return pl.pallas_call(
        kern,
        grid=(nblk,),
        in_specs=[pl.BlockSpec((TM, ka), lambda j: (j, 0)),
                  pl.BlockSpec((TM, kb), lambda j: (j, 0)),
                  pl.BlockSpec((ka, D_MODEL), lambda j: (0, 0)),
                  pl.BlockSpec((kb, D_MODEL), lambda j: (0, 0)),
                  pl.BlockSpec((TM, D_MODEL), x_map),
                  _mod_spec(layer, 2, 1),
                  pl.BlockSpec((1, D_MODEL), lambda j: (0, 0)),
                  pl.BlockSpec((1, D_MODEL), lambda j: (0, 0))],
        out_specs=pl.BlockSpec((TM, D_MODEL), lambda j: (j, 0)),
        out_shape=jax.ShapeDtypeStruct((nblk * TM, D_MODEL), F32),
        compiler_params=_cparams(("arbitrary",), 32 << 20),
        name=name,
    )(a, b, wa, wb, x, modt, g, beta)


def _ffn_ln_kernel(x_ref, sh_ref, sc_ref, gate_ref, wg_ref, wu_ref, wd_ref, g_ref, beta_ref, o_ref,
                   acc_ref, *, nbpb):
    f = pl.program_id(1)
    r = _mod_row_all(pl.program_id(0), nbpb)
    x = x_ref[...]
    u = (x * (1.0 + sc_ref[pl.ds(r, 1), :]) + sh_ref[pl.ds(r, 1), :]).astype(BF16)
    hg = _dot(u, wg_ref[...])
    hu = _dot(u, wu_ref[...])
    h = (hg * jax.nn.sigmoid(hg) * hu).astype(BF16)
    part = _dot(h, wd_ref[...])

    @pl.when(f == 0)
    def _():
        acc_ref[...] = part

    @pl.when(f > 0)
    def _():
        acc_ref[...] += part

    @pl.when(f == pl.num_programs(1) - 1)
    def _():
        gate = gate_ref[pl.ds(r, 1), :]
        o_ref[...] = _layer_norm(ALPHA * x + gate * acc_ref[...], g_ref[...], beta_ref[...])


def _ffn_ln(x1, modt, wg, wu, wd, g, beta, *, nbpb, nblk, tm, tf):
    nf = wg.shape[1] // tf
    kern = functools.partial(_ffn_ln_kernel, nbpb=nbpb * TM // tm)
    return pl.pallas_call(
        kern,
        grid=(nblk * TM // tm, nf),
        in_specs=[pl.BlockSpec((tm, D_MODEL), lambda j, f: (j, 0)),
                  _mod_spec(0, 3, 2), _mod_spec(0, 4, 2), _mod_spec(0, 5, 2),
                  pl.BlockSpec((D_MODEL, tf), lambda j, f: (0, f)),
                  pl.BlockSpec((D_MODEL, tf), lambda j, f: (0, f)),
                  pl.BlockSpec((tf, D_MODEL), lambda j, f: (f, 0)),
                  pl.BlockSpec((1, D_MODEL), lambda j, f: (0, 0)),
                  pl.BlockSpec((1, D_MODEL), lambda j, f: (0, 0))],
        out_specs=pl.BlockSpec((tm, D_MODEL), lambda j, f: (j, 0)),
        out_shape=jax.ShapeDtypeStruct((nblk * TM, D_MODEL), F32),
        scratch_shapes=[pltpu.VMEM((tm, D_MODEL), F32)],
        compiler_params=_cparams(("arbitrary", "arbitrary"), 40 << 20),
        name="even_ffn_ln",
    )(x1, modt, modt, modt, wg, wu, wd, g, beta)


def _odd_inproj_kernel(x_ref, sh_ref, sc_ref, w_ref, gq_ref, gkv_ref, wq_ref, wqp_ref, wk_ref, wv_ref,
                       pl2_ref, cq_ref, sq_ref, cs_ref,
                       qt_ref, k_ref, vt_ref, nq_ref, nk_ref, nv_ref, *, nbpb):
    j = pl.program_id(0)
    r = _mod_row_all(j, nbpb)
    u = x_ref[...] * (1.0 + sc_ref[pl.ds(r, 1), :]) + sh_ref[pl.ds(r, 1), :]
    acc = _dot(u.astype(BF16), w_ref[...])
    o = 0
    cq = acc[:, o:o + C_Q_RANK]; o += C_Q_RANK
    ckv = acc[:, o:o + C_KV_RANK]; o += C_KV_RANK
    nq = acc[:, o:o + NA_W]; o += NA_W
    nk = acc[:, o:o + NA_W]; o += NA_W
    nv = acc[:, o:o + NA_W]; o += NA_W
    krc = acc[:, o:o + LANES]

    def rms(a, g):
        return (a * lax.rsqrt(jnp.mean(a * a, axis=-1, keepdims=True) + NORM_EPS) * g).astype(BF16)

    cqn = rms(cq, gq_ref[...])
    kvn = rms(ckv, gkv_ref[...])
    q = _dot(cqn, wq_ref[...]) * cq_ref[...] + _dot(cqn, wqp_ref[...]) * sq_ref[...]
    q = q * ((C_NOPE + C_ROPE) ** -0.5 * LOG2E)
    qt_ref[...] = q.T.astype(BF16)
    kr_terms = (krc * cs_ref[...]).astype(BF16)
    k_ref[...] = (_dot(kvn, wk_ref[...]) + _dot(kr_terms, pl2_ref[...])).astype(BF16)
    vt_ref[...] = _dot(kvn, wv_ref[...]).T.astype(BF16)
    nq_ref[...] = (nq * HEAD_DIM ** -0.5).astype(BF16)
    nk_ref[...] = nk.astype(BF16)
    nv_ref[...] = nv.astype(BF16)


def _odd_inproj(xa, modt, w_ext, gq, gkv, wq_pad, wq_padp, wk_pad, wv, pl2, cos_q, sin_q, cs_k, *, batch, nt):
    nbpb = nt // TM
    nblk = batch * nbpb
    n_ext = w_ext.shape[1]
    hq = C_HEADS * LANES
    hv = C_HEADS * C_V
    kern = functools.partial(_odd_inproj_kernel, nbpb=nbpb)
    const = lambda j: (0, 0)
    tok = lambda j: (j, 0)
    pos = lambda j: (j % nbpb, 0)
    return pl.pallas_call(
        kern,
        grid=(nblk,),
        in_specs=[pl.BlockSpec((TM, D_MODEL), tok),
                  _mod_spec(1, 0, 1), _mod_spec(1, 1, 1),
                  pl.BlockSpec((D_MODEL, n_ext), const),
                  pl.BlockSpec((1, C_Q_RANK), const),
                  pl.BlockSpec((1, C_KV_RANK), const),
                  pl.BlockSpec((C_Q_RANK, hq), const),
                  pl.BlockSpec((C_Q_RANK, hq), const),
                  pl.BlockSpec((C_KV_RANK, hq), const),
                  pl.BlockSpec((C_KV_RANK, hv), const),
                  pl.BlockSpec((LANES, hq), const),
                  pl.BlockSpec((TM, hq), pos),
                  pl.BlockSpec((TM, hq), pos),
                  pl.BlockSpec((TM, LANES), pos)],
        out_specs=[pl.BlockSpec((None, hq, TM), lambda j: (j // nbpb, 0, j % nbpb)),
                   pl.BlockSpec((TM, hq), tok),
                   pl.BlockSpec((None, None, hv, TM), lambda j: (j // nbpb, j % nbpb, 0, 0)),
                   pl.BlockSpec((TM, NA_W), tok),
                   pl.BlockSpec((TM, NA_W), tok),
                   pl.BlockSpec((TM, NA_W), tok)],
        out_shape=[jax.ShapeDtypeStruct((batch, hq, nt), BF16),
                   jax.ShapeDtypeStruct((batch * nt, hq), BF16),
                   jax.ShapeDtypeStruct((batch, nbpb, hv, TM), BF16),
                   jax.ShapeDtypeStruct((batch * nt, NA_W), BF16),
                   jax.ShapeDtypeStruct((batch * nt, NA_W), BF16),
                   jax.ShapeDtypeStruct((batch * nt, NA_W), BF16)],
        compiler_params=_cparams(("arbitrary",), 48 << 20),
        name="odd_inproj",
    )(xa, modt, modt, w_ext, gq, gkv, wq_pad, wq_padp, wk_pad, wv, pl2, cos_q, sin_q, cs_k)


def _na_kernel(q_ref, kb_ref, vb_ref, kc_ref, vc_ref, bias_ref, o_ref):
    lane = lax.broadcasted_iota(jnp.int32, (GRID_W, LANES), 1)
    for hp in range(D_HEADS // 2):
        cols = slice(hp * LANES, (hp + 1) * LANES)
        qp = q_ref[:, cols]
        kb, vb, kc, vc = kb_ref[0, :, cols], vb_ref[0, :, cols], kc_ref[:, cols], vc_ref[:, cols]
        outs = []
        for hl in range(2):
            sel = (lane // HEAD_DIM) == hl
            qh = jnp.where(sel, qp, jnp.zeros_like(qp))
            s_w = _dot_nt(qh, kb) + bias_ref[2 * hp + hl]
            s_c = _dot_nt(qh, kc)
            m = jnp.maximum(jnp.max(s_w, axis=-1, keepdims=True), jnp.max(s_c, axis=-1, keepdims=True))
            p_w = jnp.exp(s_w - m)
            p_c = jnp.exp(s_c - m)
            l = jnp.sum(p_w, axis=-1, keepdims=True) + jnp.sum(p_c, axis=-1, keepdims=True)
            o = _dot(p_w.astype(BF16), vb) + _dot(p_c.astype(BF16), vc)
            outs.append(o / l)
        o_ref[:, cols] = jnp.where((lane // HEAD_DIM) == 0, outs[0], outs[1]).astype(BF16)


def _na(nq, nk, nv, bias_tab, *, batch, nt, seq):
    rows = seq // GRID_W
    kh = NA_WIN_H
    band = kh * GRID_W
    ctx_rows = TM // GRID_W

    def rs_of(r):
        return jnp.clip(r - kh // 2, 0, rows - kh)

    nq3 = nq.reshape(batch, nt, NA_W)
    nk3 = nk.reshape(batch, nt, NA_W)
    nv3 = nv.reshape(batch, nt, NA_W)
    band_spec = pl.BlockSpec((pl.Element(1), pl.Element(band), pl.Element(NA_W)),
                             lambda b, r: (b, pl.multiple_of(TM + rs_of(r) * GRID_W, GRID_W), 0))
    ctx_spec = pl.BlockSpec((None, TM, NA_W), lambda b, r: (b, 0, 0))
    return pl.pallas_call(
        _na_kernel,
        grid=(batch, rows),
        in_specs=[pl.BlockSpec((None, GRID_W, NA_W), lambda b, r: (b, ctx_rows + r, 0)),
                  band_spec, band_spec, ctx_spec, ctx_spec,
                  pl.BlockSpec((None, D_HEADS, GRID_W, band), lambda b, r: (rs_of(r) - r + NA_WIN_H - 1, 0, 0, 0))],
        out_specs=pl.BlockSpec((None, GRID_W, NA_W), lambda b, r: (b, r, 0)),
        out_shape=jax.ShapeDtypeStruct((batch, seq, NA_W), BF16),
        compiler_params=_cparams(("arbitrary", "arbitrary"), 24 << 20),
        name="neighborhood_attn",
    )(nq3, nk3, nv3, nk3, nv3, bias_tab)


def _na_bias_table(rel_bias):
    j = np.arange(GRID_W)
    col_start = np.clip(j - NA_WIN_W // 2, 0, GRID_W - NA_WIN_W)
    col_valid = (j[None, :] >= col_start[:, None]) & (j[None, :] < col_start[:, None] + NA_WIN_W)
    col_idx = np.clip(j[None, :] - j[:, None] + NA_WIN_W - 1, 0, 2 * NA_WIN_W - 2)
    ds = np.arange(NA_WIN_H)[:, None] + np.arange(NA_WIN_H)[None, :]
    tab = rel_bias[:, ds][:, :, :, col_idx]
    tab = jnp.where(col_valid[None, None, None], tab, NEG)
    tab = tab.transpose(1, 0, 3, 2, 4)
    return tab.reshape(NA_WIN_H, D_HEADS, GRID_W, NA_WIN_H * GRID_W).astype(F32)


def _router_kernel(x_ref, sh_ref, sc_ref, wh_ref, wl_ref, b_ref, tri_ref, o_ref, cnt_ref, base_ref, *, spb):
    j = pl.program_id(0)

    @pl.when(j == 0)
    def _():
        base_ref[...] = jnp.zeros_like(base_ref)

    r = 1 + j // spb
    u = x_ref[...] * (1.0 + sc_ref[pl.ds(r, 1), :]) + sh_ref[pl.ds(r, 1), :]
    u_hi = u.astype(BF16)
    u_lo = (u - u_hi.astype(F32)).astype(BF16)
    wh = wh_ref[...]
    logits = _dot(u_hi, wh) + _dot(u_lo, wh) + _dot(u_hi, wl_ref[...]) + b_ref[...]
    lane = lax.broadcasted_iota(jnp.int32, (TM, LANES), 1)
    m1 = jnp.max(logits, axis=-1, keepdims=True)
    i1 = jnp.min(jnp.where(logits == m1, lane, LANES), axis=-1, keepdims=True)
    rest = jnp.where(lane == i1, NEG * 2, logits)
    m2 = jnp.max(rest, axis=-1, keepdims=True)
    i2 = jnp.min(jnp.where(rest == m2, lane, LANES), axis=-1, keepdims=True)
    e21 = jnp.exp(m2 - m1)
    g1 = 1.0 / (1.0 + e21)
    g2 = e21 / (1.0 + e21)
    oh1 = (lane == i1).astype(F32)
    oh2 = (lane == i2).astype(F32)
    both = oh1 + oh2
    before = _dot(tri_ref[...], both.astype(BF16)) + base_ref[0:1, :]
    r1 = jnp.sum(oh1 * before, axis=-1, keepdims=True)
    r2 = jnp.sum(oh2 * before, axis=-1, keepdims=True)
    out = jnp.zeros((TM, LANES), F32)
    for idx, col in enumerate((i1.astype(F32), i2.astype(F32), g1, g2, r1, r2)):
        out = jnp.where(lane == idx, col, out)
    o_ref[...] = out
    total = base_ref[0:1, :] + jnp.sum(both, axis=0, keepdims=True)
    base_ref[...] = jnp.broadcast_to(total, base_ref.shape)
    cnt_ref[...] = jnp.broadcast_to(total, cnt_ref.shape)


def _router(x1, modt, wr_hi, wr_lo, br, tri, *, nblk, spb):
    kern = functools.partial(_router_kernel, spb=spb)
    return pl.pallas_call(
        kern,
        grid=(nblk,),
        in_specs=[pl.BlockSpec((TM, D_MODEL), lambda j: (j, 0)),
                  _mod_spec(1, 3, 1), _mod_spec(1, 4, 1),
                  pl.BlockSpec((D_MODEL, LANES), lambda j: (0, 0)),
                  pl.BlockSpec((D_MODEL, LANES), lambda j: (0, 0)),
                  pl.BlockSpec((1, LANES), lambda j: (0, 0)),
                  pl.BlockSpec((TM, TM), lambda j: (0, 0))],
        out_specs=[pl.BlockSpec((TM, LANES), lambda j: (j, 0)),
                   pl.BlockSpec((8, LANES), lambda j: (0, 0))],
        out_shape=[jax.ShapeDtypeStruct((nblk * TM, LANES), F32),
                   jax.ShapeDtypeStruct((8, LANES), F32)],
        scratch_shapes=[pltpu.VMEM((8, LANES), F32)],
        compiler_params=_cparams(("arbitrary",), 24 << 20),
        name="moe_router",
    )(x1, modt, modt, wr_hi, wr_lo, br, tri)


def _dispatch_kernel(dest_ref, x_ref, sh_ref, sc_ref, xp_in_ref, xp_ref, u_ref, sem, *, spb):
    del xp_in_ref
    j = pl.program_id(0)
    r = 1 + j // spb
    u_ref[...] = x_ref[...] * (1.0 + sc_ref[pl.ds(r, 1), :]) + sh_ref[pl.ds(r, 1), :]

    def row_copy(i, k):
        d = dest_ref[0, k * TM + i]
        return pltpu.make_async_copy(u_ref.at[pl.ds(i, 1)], xp_ref.at[pl.ds(d, 1)], sem)

    def start(i, c):
        row_copy(i, 0).start()
        row_copy(i, 1).start()
        return c

    lax.fori_loop(0, TM, start, 0)

    def wait(i, c):
        row_copy(i, 0).wait()
        row_copy(i, 1).wait()
        return c

    lax.fori_loop(0, TM, wait, 0)


def _dispatch(dest_blk, x1, modt, xpad0, *, nblk, spb):
    kern = functools.partial(_dispatch_kernel, spb=spb)
    return pl.pallas_call(
        kern,
        grid=(nblk,),
        in_specs=[pl.BlockSpec((None, 1, 2 * TM), lambda j: (j, 0, 0), memory_space=pltpu.SMEM),
                  pl.BlockSpec((TM, D_MODEL), lambda j: (j, 0)),
                  _mod_spec(1, 3, 1), _mod_spec(1, 4, 1),
                  pl.BlockSpec(memory_space=pl.ANY)],
        out_specs=pl.BlockSpec(memory_space=pl.ANY),
        out_shape=jax.ShapeDtypeStruct(xpad0.shape, F32),
        scratch_shapes=[pltpu.VMEM((TM, D_MODEL), F32), pltpu.SemaphoreType.DMA(())],
        input_output_aliases={4: 0},
        compiler_params=_cparams(("arbitrary",), 24 << 20),
        name="moe_dispatch",
    )(dest_blk, x1, modt, modt, xpad0)


def _expert_kernel(be_ref, x_ref, wg_ref, wu_ref, wd_ref, o_ref):
    del be_ref
    f = pl.program_id(1)
    xb = x_ref[...].astype(BF16)
    hg = _dot(xb, wg_ref[...])
    hu = _dot(xb, wu_ref[...])
    h = (hg * jax.nn.sigmoid(hg) * hu).astype(BF16)
    part = _dot(h, wd_ref[...])

    @pl.when(f == 0)
    def _():
        o_ref[...] = part

    @pl.when(f > 0)
    def _():
        o_ref[...] += part


def _experts(blk_expert, xpad, wg, wu, wd):
    npad = xpad.shape[0]
    nb = npad // MOE_BM
    nf = wg.shape[2] // MOE_TF
    grid_spec = pltpu.PrefetchScalarGridSpec(
        num_scalar_prefetch=1,
        grid=(nb, nf),
        in_specs=[pl.BlockSpec((MOE_BM, D_MODEL), lambda i, f, be: (i, 0)),
                  pl.BlockSpec((None, D_MODEL, MOE_TF), lambda i, f, be: (be[i], 0, f)),
                  pl.BlockSpec((None, D_MODEL, MOE_TF), lambda i, f, be: (be[i], 0, f)),
                  pl.BlockSpec((None, MOE_TF, D_MODEL), lambda i, f, be: (be[i], f, 0))],
        out_specs=pl.BlockSpec((MOE_BM, D_MODEL), lambda i, f, be: (i, 0)),
    )
    return pl.pallas_call(
        _expert_kernel,
        grid_spec=grid_spec,
        out_shape=jax.ShapeDtypeStruct((npad, D_MODEL), F32),
        compiler_params=_cparams(("arbitrary", "arbitrary"), 40 << 20),
        name="moe_experts",
    )(blk_expert, xpad, wg, wu, wd)


def _combine_kernel(dest_ref, yp_ref, x_ref, rt_ref, gate_ref, g_ref, beta_ref, o_ref, ybuf_ref, sem, *, spb):
    j = pl.program_id(0)
    r = 1 + j // spb

    def row_copy(i, k):
        d = dest_ref[0, k * TM + i]
        return pltpu.make_async_copy(yp_ref.at[pl.ds(d, 1)], ybuf_ref.at[k, pl.ds(i, 1)], sem)

    def start(i, c):
        row_copy(i, 0).start()
        row_copy(i, 1).start()
        return c

    lax.fori_loop(0, TM, start, 0)

    def wait(i, c):
        row_copy(i, 0).wait()
        row_copy(i, 1).wait()
        return c

    lax.fori_loop(0, TM, wait, 0)
    rt = rt_ref[...]
    g1 = rt[:, 2:3]
    g2 = rt[:, 3:4]
    y = g1 * ybuf_ref[0] + g2 * ybuf_ref[1]
    gate = gate_ref[pl.ds(r, 1), :]
    o_ref[...] = _layer_norm(ALPHA * x_ref[...] + gate * y, g_ref[...], beta_ref[...])


def _combine(dest_blk, ypad, x1, rt, modt, g, beta, *, nblk, spb):
    kern = functools.partial(_combine_kernel, spb=spb)
    return pl.pallas_call(
        kern,
        grid=(nblk,),
        in_specs=[pl.BlockSpec((None, 1, 2 * TM), lambda j: (j, 0, 0), memory_space=pltpu.SMEM),
                  pl.BlockSpec(memory_space=pl.ANY),
                  pl.BlockSpec((TM, D_MODEL), lambda j: (j, 0)),
                  pl.BlockSpec((TM, LANES), lambda j: (j, 0)),
                  _mod_spec(1, 5, 1),
                  pl.BlockSpec((1, D_MODEL), lambda j: (0, 0)),
                  pl.BlockSpec((1, D_MODEL), lambda j: (0, 0))],
        out_specs=pl.BlockSpec((TM, D_MODEL), lambda j: (j, 0)),
        out_shape=jax.ShapeDtypeStruct((nblk * TM, D_MODEL), F32),
        scratch_shapes=[pltpu.VMEM((2, TM, D_MODEL), F32), pltpu.SemaphoreType.DMA(())],
        compiler_params=_cparams(("arbitrary",), 24 << 20),
        name="moe_combine_ln",
    )(dest_blk, ypad, x1, rt, modt, g, beta)


def _head_perm(partner, heads, width):
    return np.concatenate([h * width + partner for h in range(heads)])


def _even_weights(w_in, q_gain, k_gain, partner):
    perm_q = _head_perm(partner, A_Q_HEADS, HEAD_DIM)
    perm_k = _head_perm(partner, A_KV_HEADS, HEAD_DIM)
    wq = w_in[:, :A_Q_W]
    wk = w_in[:, A_Q_W:A_Q_W + A_KV_W]
    w_ext = jnp.concatenate([w_in, wq[:, perm_q], wk[:, perm_k]], axis=1).astype(BF16)
    gvec = jnp.zeros((8, LANES), F32)
    gvec = gvec.at[0].set(jnp.tile(q_gain, 2)).at[1].set(jnp.tile(q_gain[partner], 2))
    gvec = gvec.at[2].set(jnp.tile(k_gain, 2)).at[3].set(jnp.tile(k_gain[partner], 2))
    return w_ext, gvec


def _block_diag(blocks):
    n = blocks.shape[0]
    w = blocks.shape[1]
    out = jnp.zeros((n * w, n * w), blocks.dtype)
    for g in range(n):
        out = out.at[g * w:(g + 1) * w, g * w:(g + 1) * w].set(blocks[g])
    return out


def _odd_weights(w_in, w_q_up, w_kv_up, partner):
    o_kr = C_Q_RANK + C_KV_RANK
    w_kr = w_in[:, o_kr:o_kr + C_ROPE]
    w_rest = jnp.concatenate([w_in[:, :o_kr], w_in[:, o_kr + C_ROPE:]], axis=1)
    kr_blk = jnp.concatenate([w_kr, w_kr[:, partner], jnp.zeros((D_MODEL, LANES - 2 * C_ROPE), F32)], axis=1)
    w_ext = jnp.concatenate([w_rest, kr_blk], axis=1).astype(BF16)
    dq = C_NOPE + C_ROPE
    wq3 = w_q_up.reshape(C_Q_RANK, C_HEADS, dq)
    zq = jnp.zeros((C_Q_RANK, C_HEADS, LANES - dq), F32)
    wq_pad = jnp.concatenate([wq3, zq], axis=-1).reshape(C_Q_RANK, C_HEADS * LANES)
    wq_rope_p = wq3[:, :, C_NOPE:][:, :, partner]
    wq_padp = jnp.concatenate([jnp.zeros((C_Q_RANK, C_HEADS, C_NOPE), F32), wq_rope_p, zq], axis=-1)
    wq_padp = wq_padp.reshape(C_Q_RANK, C_HEADS * LANES)
    wkv3 = w_kv_up.reshape(C_KV_RANK, C_HEADS, C_NOPE + C_V)
    wk_pad = jnp.concatenate([wkv3[:, :, :C_NOPE], jnp.zeros((C_KV_RANK, C_HEADS, LANES - C_NOPE), F32)], axis=-1)
    wk_pad = wk_pad.reshape(C_KV_RANK, C_HEADS * LANES)
    wv = wkv3[:, :, C_NOPE:].reshape(C_KV_RANK, C_HEADS * C_V)
    pl2 = np.zeros((LANES, C_HEADS * LANES), np.float32)
    for h in range(C_HEADS):
        for dd in range(C_ROPE):
            pl2[dd, h * LANES + C_NOPE + dd] = 1.0
            pl2[C_ROPE + dd, h * LANES + C_NOPE + dd] = 1.0
    return (w_ext, wq_pad.astype(BF16), wq_padp.astype(BF16), wk_pad.astype(BF16), wv.astype(BF16),
            jnp.asarray(pl2, BF16))


def kernel(x, c, ctx, c_ctx, w_ada, b_ada, ln1_g, ln1_b, ln2_g, ln2_b,
           ev_w_in, ev_w_out, ev_q_gain, ev_k_gain, ev_w_pool, ev_pool_scale,
           ev_w_gate, ev_w_up, ev_w_down,
           od_w_in, od_w_out, od_q_lat_gain, od_kv_lat_gain, od_w_q_up, od_w_kv_up, od_na_bias,
           od_w_router, od_b_router, od_w_gate, od_w_up, od_w_down):
    batch, seq, d = x.shape
    ctx_len = ctx.shape[1]
    assert d == D_MODEL and ctx_len == TM and seq % TM == 0 and seq % GRID_W == 0
    assert seq // GRID_W >= NA_WIN_H and batch + 1 <= 8
    nt = ctx_len + seq
    nbpb = nt // TM
    nblk = batch * nbpb
    spb = seq // TM
    nxb = batch * spb

    xa = jnp.concatenate([ctx, x], axis=1).reshape(batch * nt, d)
    cin = jnp.zeros((8, d), F32).at[0].set(c_ctx).at[1:1 + batch].set(c)
    modt = _ada_table(cin, w_ada, b_ada)

    cos64, sin64, partner64 = _rope_axis_tables(HEAD_DIM, seq, ctx_len)
    w_ext0, gvec = _even_weights(ev_w_in[0], ev_q_gain[0], ev_k_gain[0], partner64)
    cos2 = jnp.tile(cos64, (1, 2))
    sin2 = jnp.tile(sin64, (1, 2))
    ones_bd = _block_diag(jnp.ones((2, HEAD_DIM, HEAD_DIM), BF16))
    qt, kk, vt, pp = _even_inproj(xa, modt, w_ext0, cos2, sin2, gvec, ones_bd, batch=batch, nt=nt)
    attn = _flash(qt, kk.reshape(batch, nt, A_KV_W), vt, batch=batch, nt=nt, units=A_KV_HEADS // 2,
                  nq=2 * A_GROUP, kw=LANES, k_off=(0,) * (2 * A_GROUP),
                  v_off=tuple((jq // A_GROUP) * HEAD_DIM for jq in range(2 * A_GROUP)),
                  ctx_queries=True, name="gqa_flash")
    pooled = _pool(pp, _block_diag(ev_w_pool[0]).astype(BF16), ev_pool_scale[0].reshape(1, POOL_WIDTH),
                   batch=batch, nt=nt)
    w_out0 = ev_w_out[0].astype(BF16)
    x1 = _proj_ln(attn.reshape(batch * nt, A_Q_W), pooled, w_out0[:A_Q_W], w_out0[A_Q_W:], xa, modt,
                  ln1_g[0].reshape(1, d), ln1_b[0].reshape(1, d), layer=0, nblk=nblk,
                  x_map=lambda j: (j, 0), row_fn=lambda j: _mod_row_all(j, nbpb), name="even_out_ln")
    xa1 = _ffn_ln(x1, modt, ev_w_gate[0].astype(BF16), ev_w_up[0].astype(BF16), ev_w_down[0].astype(BF16),
                  ln2_g[0].reshape(1, d), ln2_b[0].reshape(1, d), nbpb=nbpb, nblk=nblk, tm=TM, tf=D_FF // 2)

    cos32, sin32, partner32 = _rope_axis_tables(C_ROPE, seq, ctx_len)
    w_ext1, wq_pad, wq_padp, wk_pad, wv, pl2 = _odd_weights(od_w_in[0], od_w_q_up[0], od_w_kv_up[0], partner32)
    ntr = cos32.shape[0]
    pad_tail = jnp.zeros((ntr, LANES - C_NOPE - C_ROPE), F32)
    cos_q = jnp.tile(jnp.concatenate([jnp.ones((ntr, C_NOPE), F32), cos32, pad_tail], axis=1), (1, C_HEADS))
    sin_q = jnp.tile(jnp.concatenate([jnp.zeros((ntr, C_NOPE), F32), sin32, pad_tail], axis=1), (1, C_HEADS))
    cs_k = jnp.concatenate([cos32, sin32, jnp.zeros((ntr, LANES - 2 * C_ROPE), F32)], axis=1)
    qt1, k1, vt1, nq, nk, nv = _odd_inproj(
        xa1, modt, w_ext1, od_q_lat_gain[0].reshape(1, C_Q_RANK), od_kv_lat_gain[0].reshape(1, C_KV_RANK),
        wq_pad, wq_padp, wk_pad, wv, pl2, cos_q, sin_q, cs_k, batch=batch, nt=nt)
    mla = _flash(qt1, k1.reshape(batch, nt, C_HEADS * LANES), vt1, batch=batch, nt=nt, units=C_HEADS // 2,
                 nq=2, kw=2 * LANES, k_off=(0, LANES), v_off=(0, C_V), ctx_queries=False, name="mla_flash")
    na = _na(nq, nk, nv, _na_bias_table(od_na_bias[0]), batch=batch, nt=nt, seq=seq)
    w_out1 = od_w_out[0].astype(BF16)
    x1b = _proj_ln(mla.reshape(batch * seq, C_HEADS * C_V), na.reshape(batch * seq, NA_W),
                   w_out1[:C_HEADS * C_V], w_out1[C_HEADS * C_V:], xa1, modt,
                   ln1_g[1].reshape(1, d), ln1_b[1].reshape(1, d), layer=1, nblk=nxb,
                   x_map=lambda j: (j + j // spb + 1, 0), row_fn=lambda j: 1 + j // spb, name="odd_out_ln")

    wr = jnp.zeros((d, LANES), F32).at[:, :N_EXPERTS].set(od_w_router[0])
    wr_hi = wr.astype(BF16)
    wr_lo = (wr - wr_hi.astype(F32)).astype(BF16)
    br = jnp.full((1, LANES), NEG, F32).at[0, :N_EXPERTS].set(od_b_router[0])
    tri = jnp.asarray(np.tril(np.ones((TM, TM), np.float32), -1), BF16)
    rt, cnt = _router(x1b, modt, wr_hi, wr_lo, br, tri, nblk=nxb, spb=spb)

    t_tok = batch * seq
    counts = cnt[0, :N_EXPERTS].astype(jnp.int32)
    padded = (counts + MOE_BM - 1) // MOE_BM * MOE_BM
    pad_end = jnp.cumsum(padded)
    pad_start = pad_end - padded
    n_blocks = -(-2 * t_tok // MOE_BM) + N_EXPERTS
    e12 = rt[:, 0:2].astype(jnp.int32)
    r12 = rt[:, 4:6].astype(jnp.int32)
    dest = pad_start[e12] + r12
    dest_blk = dest.reshape(nxb, TM, 2).transpose(0, 2, 1).reshape(nxb, 1, 2 * TM)
    blk_expert = jnp.minimum(
        jnp.searchsorted(pad_end, jnp.arange(n_blocks, dtype=jnp.int32) * MOE_BM, side='right'),
        N_EXPERTS - 1).astype(jnp.int32)

    xpad = _dispatch(dest_blk, x1b, modt, jnp.zeros((n_blocks * MOE_BM, d), F32), nblk=nxb, spb=spb)
    ypad = _experts(blk_expert, xpad, od_w_gate[0].astype(BF16), od_w_up[0].astype(BF16),
                    od_w_down[0].astype(BF16))
    out = _combine(dest_blk, ypad, x1b, rt, modt, ln2_g[1].reshape(1, d), ln2_b[1].reshape(1, d),
                   nblk=nxb, spb=spb)
    return out.reshape(batch, seq, d)
```

```python
import functools
import math

import numpy as np
import jax
import jax.numpy as jnp
from jax import lax
from jax.experimental import pallas as pl
from jax.experimental.pallas import tpu as pltpu

F32 = jnp.float32
BF16 = jnp.bfloat16

D_MODEL = 1024
DEPTH = 2
GRID_W = 64
ROPE_THETA = 10000.0
NORM_EPS = 1e-6
ALPHA = (2 * DEPTH) ** 0.25
HEAD_DIM = 64
A_Q_HEADS = 12
A_KV_HEADS = 4
A_GROUP = A_Q_HEADS // A_KV_HEADS
POOL_WINDOWS = (2, 4, 8, 16)
POOL_GROUP = 64
POOL_WIDTH = POOL_GROUP * len(POOL_WINDOWS)
C_HEADS = 8
C_Q_RANK = 384
C_KV_RANK = 256
C_NOPE = 64
C_ROPE = 32
C_V = 64
D_HEADS = 8
NA_WIN_H = 8
NA_WIN_W = 16
D_FF = 2816
N_EXPERTS = 8
D_FF_EXPERT = 3584
A_Q_W = A_Q_HEADS * HEAD_DIM
A_KV_W = A_KV_HEADS * HEAD_DIM
NA_W = D_HEADS * HEAD_DIM

LANES = 128
TM = 256
VMEM_CAP = 56 * 1024 * 1024
MOE_BM = 512
MOE_TF = 512
FLASH_STAGES = 24
FLASH_LOOKAHEAD = 4
NEG = -1e30
LOG2E = 1.4426950408889634


def _cparams(sem, vmem_bytes):
    return pltpu.CompilerParams(dimension_semantics=sem,
                                vmem_limit_bytes=int(min(VMEM_CAP, max(vmem_bytes, 16 * 1024 * 1024))))


def _dot(a, b):
    return jnp.dot(a, b, preferred_element_type=F32)


def _dot_nt(a, b):
    return lax.dot_general(a, b, (((1,), (1,)), ((), ())), preferred_element_type=F32)


def _layer_norm(z, g, b):
    mu = jnp.mean(z, axis=-1, keepdims=True)
    zc = z - mu
    var = jnp.mean(zc * zc, axis=-1, keepdims=True)
    return zc * lax.rsqrt(var + NORM_EPS) * g + b


def _ada_kernel(c_ref, w_ref, b_ref, o_ref):
    c = c_ref[...]
    s = c * jax.nn.sigmoid(c)
    o_ref[...] = _dot(s.astype(BF16), w_ref[...].astype(BF16)) + b_ref[...]


def _ada_table(cin, w_ada, b_ada):
    depth, d, n = w_ada.shape
    return pl.pallas_call(
        _ada_kernel,
        grid=(depth, n // d),
        in_specs=[pl.BlockSpec((8, d), lambda l, j: (0, 0)),
                  pl.BlockSpec((None, d, d), lambda l, j: (l, 0, j)),
                  pl.BlockSpec((None, 1, d), lambda l, j: (l, 0, j))],
        out_specs=pl.BlockSpec((None, 8, d), lambda l, j: (l, 0, j)),
        out_shape=jax.ShapeDtypeStruct((depth, 8, n), F32),
        compiler_params=_cparams(("arbitrary", "arbitrary"), 24 << 20),
        name="ada_table",
    )(cin, w_ada, b_ada.reshape(depth, 1, n))


def _mod_spec(layer, k, nargs):
    if nargs == 1:
        return pl.BlockSpec((None, 8, D_MODEL), lambda j: (layer, 0, k))
    return pl.BlockSpec((None, 8, D_MODEL), lambda *_: (layer, 0, k))


def _mod_row_all(j, nbpb):
    return jnp.where(j % nbpb == 0, 0, 1 + j // nbpb)


def _rope_axis_tables(rope_dims, seq, ctx):
    n = rope_dims // 2
    half = n // 2
    t = jnp.arange(seq, dtype=jnp.int32)
    row = (t // GRID_W).astype(F32)
    col = (t % GRID_W).astype(F32)
    inv = ROPE_THETA ** (-jnp.arange(half, dtype=F32) / half)
    cols_c, cols_s = [], []
    for pos in (row, col):
        ang = pos[:, None] * inv[None, :]
        c, s = jnp.cos(ang), jnp.sin(ang)
        cols_c += [c, c]
        cols_s += [-s, s]
    cos = jnp.concatenate(cols_c, axis=-1)
    sin = jnp.concatenate(cols_s, axis=-1)
    cos = jnp.concatenate([jnp.ones((ctx, rope_dims), F32), cos], axis=0)
    sin = jnp.concatenate([jnp.zeros((ctx, rope_dims), F32), sin], axis=0)
    d = np.arange(rope_dims)
    partner = (d // n) * n + ((d % n) + half) % n
    return cos, sin, partner


def _even_inproj_kernel(x_ref, sh_ref, sc_ref, w_ref, cos_ref, sin_ref, gv_ref, ones_ref,
                        qt_ref, k_ref, vt_ref, p_ref, *, nbpb):
    j = pl.program_id(0)
    r = _mod_row_all(j, nbpb)
    shift = sh_ref[pl.ds(r, 1), :]
    scale = sc_ref[pl.ds(r, 1), :]
    u = x_ref[...] * (1.0 + scale) + shift
    acc = _dot(u.astype(BF16), w_ref[...])
    cos = cos_ref[...]
    sin = sin_ref[...]
    ones_bd = ones_ref[...]

    def norm_rope(a, ap, g, gp):
        ssq = _dot((a * a).astype(BF16), ones_bd)
        rinv = lax.rsqrt(ssq * (1.0 / HEAD_DIM) + NORM_EPS)
        return (a * g * cos + ap * gp * sin) * rinv

    gq, gqp, gk, gkp = gv_ref[0:1, :], gv_ref[1:2, :], gv_ref[2:3, :], gv_ref[3:4, :]
    p_off = A_Q_W + 2 * A_KV_W + POOL_WIDTH
    zeros_half = jnp.zeros((HEAD_DIM, TM), BF16)
    for c in range(A_Q_W // LANES):
        a = acc[:, c * LANES:(c + 1) * LANES]
        ap = acc[:, p_off + c * LANES:p_off + (c + 1) * LANES]
        rot = norm_rope(a, ap, gq, gqp) * (HEAD_DIM ** -0.5 * LOG2E)
        rot_t = rot.T
        for hl in range(2):
            head = 2 * c + hl
            slot = (head // A_GROUP) % 2
            base = head * LANES
            qt_ref[base + slot * HEAD_DIM:base + (slot + 1) * HEAD_DIM, :] = (
                rot_t[hl * HEAD_DIM:(hl + 1) * HEAD_DIM, :].astype(BF16))
            qt_ref[base + (1 - slot) * HEAD_DIM:base + (2 - slot) * HEAD_DIM, :] = zeros_half
    for c in range(A_KV_W // LANES):
        a = acc[:, A_Q_W + c * LANES:A_Q_W + (c + 1) * LANES]
        ap = acc[:, p_off + A_Q_W + c * LANES:p_off + A_Q_W + (c + 1) * LANES]
        k_ref[:, c * LANES:(c + 1) * LANES] = norm_rope(a, ap, gk, gkp).astype(BF16)
    v = acc[:, A_Q_W + A_KV_W:A_Q_W + 2 * A_KV_W]
    vt_ref[...] = v.T.astype(BF16)
    p_ref[...] = acc[:, A_Q_W + 2 * A_KV_W:A_Q_W + 2 * A_KV_W + POOL_WIDTH]


def _even_inproj(xa, modt, w_ext, cos2, sin2, gvec, ones_bd, *, batch, nt):
    nbpb = nt // TM
    nblk = batch * nbpb
    n_ext = w_ext.shape[1]
    kern = functools.partial(_even_inproj_kernel, nbpb=nbpb)
    return pl.pallas_call(
        kern,
        grid=(nblk,),
        in_specs=[pl.BlockSpec((TM, D_MODEL), lambda j: (j, 0)),
                  _mod_spec(0, 0, 1), _mod_spec(0, 1, 1),
                  pl.BlockSpec((D_MODEL, n_ext), lambda j: (0, 0)),
                  pl.BlockSpec((TM, LANES), lambda j: (j % nbpb, 0)),
                  pl.BlockSpec((TM, LANES), lambda j: (j % nbpb, 0)),
                  pl.BlockSpec((8, LANES), lambda j: (0, 0)),
                  pl.BlockSpec((LANES, LANES), lambda j: (0, 0))],
        out_specs=[pl.BlockSpec((None, A_Q_HEADS * LANES, TM), lambda j: (j // nbpb, 0, j % nbpb)),
                   pl.BlockSpec((TM, A_KV_W), lambda j: (j, 0)),
                   pl.BlockSpec((None, None, A_KV_W, TM), lambda j: (j // nbpb, j % nbpb, 0, 0)),
                   pl.BlockSpec((TM, POOL_WIDTH), lambda j: (j, 0))],
        out_shape=[jax.ShapeDtypeStruct((batch, A_Q_HEADS * LANES, nt), BF16),
                   jax.ShapeDtypeStruct((batch * nt, A_KV_W), BF16),
                   jax.ShapeDtypeStruct((batch, nbpb, A_KV_W, TM), BF16),
                   jax.ShapeDtypeStruct((batch * nt, POOL_WIDTH), F32)],
        compiler_params=_cparams(("arbitrary",), 40 << 20),
        name="even_inproj",
    )(xa, modt, modt, w_ext, cos2, sin2, gvec, ones_bd)


def _flash_kernel(q_ref, k_ref, v_ref, o_ref, acc_ref, m_ref, l_ref, *, nq, k_off, v_off, ctx_queries, nsub):
    nch = v_ref.shape[0]
    acc_ref[...] = jnp.zeros_like(acc_ref)
    m_ref[...] = jnp.full_like(m_ref, NEG)
    l_ref[...] = jnp.zeros_like(l_ref)

    def scores(j, c):
        start = pl.multiple_of(c * TM, TM)
        kc = k_ref[pl.ds(start, TM), k_off[j]:k_off[j] + LANES]
        return _dot(kc, q_ref[j * LANES:(j + 1) * LANES, :])

    def update(j, c, s):
        row = slice(j, j + 1)
        m_old = m_ref[row, :]
        m_new = jnp.maximum(m_old, jnp.max(s, axis=0, keepdims=True))
        alpha = jnp.exp2(m_old - m_new)
        p = jnp.exp2(s - m_new)
        l_ref[row, :] = alpha * l_ref[row, :] + jnp.sum(p, axis=0, keepdims=True)
        m_ref[row, :] = m_new
        pv = _dot(v_ref[c, v_off[j]:v_off[j] + HEAD_DIM, :], p.astype(BF16))
        rows = slice(j * HEAD_DIM, (j + 1) * HEAD_DIM)
        acc_ref[rows, :] = alpha * acc_ref[rows, :] + pv

    def run(chunks):
        stages = [(j, c) for c in chunks for j in range(nq)]
        ahead = [scores(*st) for st in stages[:FLASH_LOOKAHEAD]]
        for i, (j, c) in enumerate(stages):
            if i + FLASH_LOOKAHEAD < len(stages):
                ahead.append(scores(*stages[i + FLASH_LOOKAHEAD]))
            update(j, c, ahead.pop(0))

    run([0])
    n_main = (nch - 1) // nsub
    if ctx_queries:
        n_main = jnp.where(pl.program_id(2) == 0, 0, n_main)

    def body(it, carry):
        run([1 + it * nsub + t for t in range(nsub)])
        return carry

    lax.fori_loop(0, n_main, body, 0)
    for j in range(nq):
        rows = slice(j * HEAD_DIM, (j + 1) * HEAD_DIM)
        acc_ref[rows, :] = acc_ref[rows, :] / l_ref[j:j + 1, :]
    o_ref[...] = acc_ref[...].T.astype(BF16)


def _flash(qt, k, vt, *, batch, nt, units, nq, kw, k_off, v_off, ctx_queries, name):
    nbpb = nt // TM
    nqb = nbpb if ctx_queries else nbpb - 1
    qoff = 0 if ctx_queries else 1
    nv = vt.shape[2] // units
    nsub = max(d for d in range(1, FLASH_STAGES // nq + 1) if (nbpb - 1) % d == 0)
    kern = functools.partial(_flash_kernel, nq=nq, k_off=k_off, v_off=v_off, ctx_queries=ctx_queries, nsub=nsub)
    vmem = 2 * (nt * kw * 2 + nt * nv * 2) + 4 * nq * LANES * TM * 2 + 3 * nq * HEAD_DIM * TM * 4 + (8 << 20)
    return pl.pallas_call(
        kern,
        grid=(batch, units, nqb),
        in_specs=[pl.BlockSpec((None, nq * LANES, TM), lambda b, u, i: (b, u, i + qoff)),
                  pl.BlockSpec((None, nt, kw), lambda b, u, i: (b, 0, u)),
                  pl.BlockSpec((None, nbpb, nv, TM), lambda b, u, i: (b, 0, u, 0))],
        out_specs=pl.BlockSpec((None, TM, nq * HEAD_DIM), lambda b, u, i: (b, i, u)),
        out_shape=jax.ShapeDtypeStruct((batch, nqb * TM, units * nq * HEAD_DIM), BF16),
        scratch_shapes=[pltpu.VMEM((nq * HEAD_DIM, TM), F32),
                        pltpu.VMEM((8, TM), F32), pltpu.VMEM((8, TM), F32)],
        compiler_params=_cparams(("arbitrary", "arbitrary", "arbitrary"), vmem),
        name=name,
    )(qt, k, vt)


def _pool_kernel(prev_ref, cur_ref, next_ref, w_ref, ps_ref, o_ref, *, nbpb, nt):
    jj = pl.program_id(0) % nbpb
    base = jj * TM
    seg_lo = jnp.where(jj == 0, 0, TM)
    seg_hi = jnp.where(jj == 0, TM, nt)
    cur = cur_ref[...]
    xcat = jnp.concatenate([prev_ref[...], cur, next_ref[...]], axis=0)
    x_hi = xcat.astype(BF16)
    x_lo = (xcat - x_hi.astype(F32)).astype(BF16)
    t = base + lax.broadcasted_iota(jnp.int32, (TM, 3 * TM), 0)
    s = base - TM + lax.broadcasted_iota(jnp.int32, (TM, 3 * TM), 1)
    t1 = base + lax.broadcasted_iota(jnp.int32, (TM, 1), 0)
    lane_grp = lax.broadcasted_iota(jnp.int32, (TM, POOL_WIDTH), 1) // POOL_GROUP
    mean = jnp.zeros((TM, POOL_WIDTH), F32)
    for g, win in enumerate(POOL_WINDOWS):
        lo = win // 2
        hi = win - lo
        valid = ((s >= jnp.maximum(t - lo, seg_lo)) & (s < jnp.minimum(t + hi, seg_hi)))
        vb = valid.astype(F32).astype(BF16)
        wsum = _dot(vb, x_hi) + _dot(vb, x_lo)
        cnt = (jnp.minimum(t1 + hi, seg_hi) - jnp.maximum(t1 - lo, seg_lo)).astype(F32)
        mean = jnp.where(lane_grp == g, wsum / cnt, mean)
    y = (mean - cur).astype(BF16)
    o_ref[...] = (_dot(y, w_ref[...]) * ps_ref[...]).astype(BF16)


def _pool(p, w_bd, ps, *, batch, nt):
    nbpb = nt // TM
    nblk = batch * nbpb
    kern = functools.partial(_pool_kernel, nbpb=nbpb, nt=nt)

    def prev_map(j):
        return (jnp.maximum(j - 1, 0), 0)

    def next_map(j):
        return (jnp.minimum(j + 1, nblk - 1), 0)

    return pl.pallas_call(
        kern,
        grid=(nblk,),
        in_specs=[pl.BlockSpec((TM, POOL_WIDTH), prev_map),
                  pl.BlockSpec((TM, POOL_WIDTH), lambda j: (j, 0)),
                  pl.BlockSpec((TM, POOL_WIDTH), next_map),
                  pl.BlockSpec((POOL_WIDTH, POOL_WIDTH), lambda j: (0, 0)),
                  pl.BlockSpec((1, POOL_WIDTH), lambda j: (0, 0))],
        out_specs=pl.BlockSpec((TM, POOL_WIDTH), lambda j: (j, 0)),
        out_shape=jax.ShapeDtypeStruct((batch * nt, POOL_WIDTH), BF16),
        compiler_params=_cparams(("arbitrary",), 24 << 20),
        name="pool_mixer",
    )(p, p, p, w_bd, ps)


def _proj_ln_kernel(a_ref, b_ref, wa_ref, wb_ref, x_ref, gate_ref, g_ref, beta_ref, o_ref, *, row_fn):
    r = row_fn(pl.program_id(0))
    gate = gate_ref[pl.ds(r, 1), :]
    y = _dot(a_ref[...], wa_ref[...]) + _dot(b_ref[...], wb_ref[...])
    o_ref[...] = _layer_norm(ALPHA * x_ref[...] + gate * y, g_ref[...], beta_ref[...])


def _proj_ln(a, b, wa, wb, x, modt, g, beta, *, layer, nblk, x_map, row_fn, name):
    ka, kb = a.shape[1], b.shape[1]
    kern = functools.partial(_proj_ln_kernel, row_fn=row_fn)
    return pl.pallas_call(
        kern,
        grid=(nblk,),
        in_specs=[pl.BlockSpec((TM, ka), lambda j: (j, 0)),
                  pl.BlockSpec((TM, kb), lambda j: (j, 0)),
                  pl.BlockSpec((ka, D_MODEL), lambda j: (0, 0)),
                  pl.BlockSpec((kb, D_MODEL), lambda j: (0, 0)),
                  pl.BlockSpec((TM, D_MODEL), x_map),
                  _mod_spec(layer, 2, 1),
                  pl.BlockSpec((1, D_MODEL), lambda j: (0, 0)),
                  pl.BlockSpec((1, D_MODEL), lambda j: (0, 0))],
        out_specs=pl.BlockSpec((TM, D_MODEL), lambda j: (j, 0)),
        out_shape=jax.ShapeDtypeStruct((nblk * TM, D_MODEL), F32),
        compiler_params=_cparams(("arbitrary",), 32 << 20),
        name=name,
    )(a, b, wa, wb, x, modt, g, beta)


def _ffn_ln_kernel(x_ref, sh_ref, sc_ref, gate_ref, wg_ref, wu_ref, wd_ref, g_ref, beta_ref, o_ref,
                   acc_ref, *, nbpb):
    f = pl.program_id(1)
    r = _mod_row_all(pl.program_id(0), nbpb)
    x = x_ref[...]
    u = (x * (1.0 + sc_ref[pl.ds(r, 1), :]) + sh_ref[pl.ds(r, 1), :]).astype(BF16)
    hg = _dot(u, wg_ref[...])
    hu = _dot(u, wu_ref[...])
    h = (hg * jax.nn.sigmoid(hg) * hu).astype(BF16)
    part = _dot(h, wd_ref[...])

    @pl.when(f == 0)
    def _():
        acc_ref[...] = part

    @pl.when(f > 0)
    def _():
        acc_ref[...] += part

    @pl.when(f == pl.num_programs(1) - 1)
    def _():
        gate = gate_ref[pl.ds(r, 1), :]
        o_ref[...] = _layer_norm(ALPHA * x + gate * acc_ref[...], g_ref[...], beta_ref[...])


def _ffn_ln(x1, modt, wg, wu, wd, g, beta, *, nbpb, nblk, tm, tf):
    nf = wg.shape[1] // tf
    kern = functools.partial(_ffn_ln_kernel, nbpb=nbpb * TM // tm)
    return pl.pallas_call(
        kern,
        grid=(nblk * TM // tm, nf),
        in_specs=[pl.BlockSpec((tm, D_MODEL), lambda j, f: (j, 0)),
                  _mod_spec(0, 3, 2), _mod_spec(0, 4, 2), _mod_spec(0, 5, 2),
                  pl.BlockSpec((D_MODEL, tf), lambda j, f: (0, f)),
                  pl.BlockSpec((D_MODEL, tf), lambda j, f: (0, f)),
                  pl.BlockSpec((tf, D_MODEL), lambda j, f: (f, 0)),
                  pl.BlockSpec((1, D_MODEL), lambda j, f: (0, 0)),
                  pl.BlockSpec((1, D_MODEL), lambda j, f: (0, 0))],
        out_specs=pl.BlockSpec((tm, D_MODEL), lambda j, f: (j, 0)),
        out_shape=jax.ShapeDtypeStruct((nblk * TM, D_MODEL), F32),
        scratch_shapes=[pltpu.VMEM((tm, D_MODEL), F32)],
        compiler_params=_cparams(("arbitrary", "arbitrary"), 40 << 20),
        name="even_ffn_ln",
    )(x1, modt, modt, modt, wg, wu, wd, g, beta)


def _odd_inproj_kernel(x_ref, sh_ref, sc_ref, w_ref, gq_ref, gkv_ref, wq_ref, wqp_ref, wk_ref, wv_ref,
                       pl2_ref, cq_ref, sq_ref, cs_ref,
                       qt_ref, k_ref, vt_ref, nq_ref, nk_ref, nv_ref, *, nbpb):
    j = pl.program_id(0)
    r = _mod_row_all(j, nbpb)
    u = x_ref[...] * (1.0 + sc_ref[pl.ds(r, 1), :]) + sh_ref[pl.ds(r, 1), :]
    acc = _dot(u.astype(BF16), w_ref[...])
    o = 0
    cq = acc[:, o:o + C_Q_RANK]; o += C_Q_RANK
    ckv = acc[:, o:o + C_KV_RANK]; o += C_KV_RANK
    nq = acc[:, o:o + NA_W]; o += NA_W
    nk = acc[:, o:o + NA_W]; o += NA_W
    nv = acc[:, o:o + NA_W]; o += NA_W
    krc = acc[:, o:o + LANES]

    def rms(a, g):
        return (a * lax.rsqrt(jnp.mean(a * a, axis=-1, keepdims=True) + NORM_EPS) * g).astype(BF16)

    cqn = rms(cq, gq_ref[...])
    kvn = rms(ckv, gkv_ref[...])
    q = _dot(cqn, wq_ref[...]) * cq_ref[...] + _dot(cqn, wqp_ref[...]) * sq_ref[...]
    q = q * ((C_NOPE + C_ROPE) ** -0.5 * LOG2E)
    qt_ref[...] = q.T.astype(BF16)
    kr_terms = (krc * cs_ref[...]).astype(BF16)
    k_ref[...] = (_dot(kvn, wk_ref[...]) + _dot(kr_terms, pl2_ref[...])).astype(BF16)
    vt_ref[...] = _dot(kvn, wv_ref[...]).T.astype(BF16)
    nq_ref[...] = (nq * HEAD_DIM ** -0.5).astype(BF16)
    nk_ref[...] = nk.astype(BF16)
    nv_ref[...] = nv.astype(BF16)


def _odd_inproj(xa, modt, w_ext, gq, gkv, wq_pad, wq_padp, wk_pad, wv, pl2, cos_q, sin_q, cs_k, *, batch, nt):
    nbpb = nt // TM
    nblk = batch * nbpb
    n_ext = w_ext.shape[1]
    hq = C_HEADS * LANES
    hv = C_HEADS * C_V
    kern = functools.partial(_odd_inproj_kernel, nbpb=nbpb)
    const = lambda j: (0, 0)
    tok = lambda j: (j, 0)
    pos = lambda j: (j % nbpb, 0)
    return pl.pallas_call(
        kern,
        grid=(nblk,),
        in_specs=[pl.BlockSpec((TM, D_MODEL), tok),
                  _mod_spec(1, 0, 1), _mod_spec(1, 1, 1),
                  pl.BlockSpec((D_MODEL, n_ext), const),
                  pl.BlockSpec((1, C_Q_RANK), const),
                  pl.BlockSpec((1, C_KV_RANK), const),
                  pl.BlockSpec((C_Q_RANK, hq), const),
                  pl.BlockSpec((C_Q_RANK, hq), const),
                  pl.BlockSpec((C_KV_RANK, hq), const),
                  pl.BlockSpec((C_KV_RANK, hv), const),
                  pl.BlockSpec((LANES, hq), const),
                  pl.BlockSpec((TM, hq), pos),
                  pl.BlockSpec((TM, hq), pos),
                  pl.BlockSpec((TM, LANES), pos)],
        out_specs=[pl.BlockSpec((None, hq, TM), lambda j: (j // nbpb, 0, j % nbpb)),
                   pl.BlockSpec((TM, hq), tok),
                   pl.BlockSpec((None, None, hv, TM), lambda j: (j // nbpb, j % nbpb, 0, 0)),
                   pl.BlockSpec((TM, NA_W), tok),
                   pl.BlockSpec((TM, NA_W), tok),
                   pl.BlockSpec((TM, NA_W), tok)],
        out_shape=[jax.ShapeDtypeStruct((batch, hq, nt), BF16),
                   jax.ShapeDtypeStruct((batch * nt, hq), BF16),
                   jax.ShapeDtypeStruct((batch, nbpb, hv, TM), BF16),
                   jax.ShapeDtypeStruct((batch * nt, NA_W), BF16),
                   jax.ShapeDtypeStruct((batch * nt, NA_W), BF16),
                   jax.ShapeDtypeStruct((batch * nt, NA_W), BF16)],
        compiler_params=_cparams(("arbitrary",), 48 << 20),
        name="odd_inproj",
    )(xa, modt, modt, w_ext, gq, gkv, wq_pad, wq_padp, wk_pad, wv, pl2, cos_q, sin_q, cs_k)


def _na_kernel(q_ref, kb_ref, vb_ref, kc_ref, vc_ref, bias_ref, o_ref):
    lane = lax.broadcasted_iota(jnp.int32, (GRID_W, LANES), 1)
    for hp in range(D_HEADS // 2):
        cols = slice(hp * LANES, (hp + 1) * LANES)
        qp = q_ref[:, cols]
        kb, vb, kc, vc = kb_ref[0, :, cols], vb_ref[0, :, cols], kc_ref[:, cols], vc_ref[:, cols]
        outs = []
        for hl in range(2):
            sel = (lane // HEAD_DIM) == hl
            qh = jnp.where(sel, qp, jnp.zeros_like(qp))
            s_w = _dot_nt(qh, kb) + bias_ref[2 * hp + hl]
            s_c = _dot_nt(qh, kc)
            m = jnp.maximum(jnp.max(s_w, axis=-1, keepdims=True), jnp.max(s_c, axis=-1, keepdims=True))
            p_w = jnp.exp(s_w - m)
            p_c = jnp.exp(s_c - m)
            l = jnp.sum(p_w, axis=-1, keepdims=True) + jnp.sum(p_c, axis=-1, keepdims=True)
            o = _dot(p_w.astype(BF16), vb) + _dot(p_c.astype(BF16), vc)
            outs.append(o / l)
        o_ref[:, cols] = jnp.where((lane // HEAD_DIM) == 0, outs[0], outs[1]).astype(BF16)


def _na(nq, nk, nv, bias_tab, *, batch, nt, seq):
    rows = seq // GRID_W
    kh = NA_WIN_H
    band = kh * GRID_W
    ctx_rows = TM // GRID_W

    def rs_of(r):
        return jnp.clip(r - kh // 2, 0, rows - kh)

    nq3 = nq.reshape(batch, nt, NA_W)
    nk3 = nk.reshape(batch, nt, NA_W)
    nv3 = nv.reshape(batch, nt, NA_W)
    band_spec = pl.BlockSpec((pl.Element(1), pl.Element(band), pl.Element(NA_W)),
                             lambda b, r: (b, pl.multiple_of(TM + rs_of(r) * GRID_W, GRID_W), 0))
    ctx_spec = pl.BlockSpec((None, TM, NA_W), lambda b, r: (b, 0, 0))
    return pl.pallas_call(
        _na_kernel,
        grid=(batch, rows),
        in_specs=[pl.BlockSpec((None, GRID_W, NA_W), lambda b, r: (b, ctx_rows + r, 0)),
                  band_spec, band_spec, ctx_spec, ctx_spec,
                  pl.BlockSpec((None, D_HEADS, GRID_W, band), lambda b, r: (rs_of(r) - r + NA_WIN_H - 1, 0, 0, 0))],
        out_specs=pl.BlockSpec((None, GRID_W, NA_W), lambda b, r: (b, r, 0)),
        out_shape=jax.ShapeDtypeStruct((batch, seq, NA_W), BF16),
        compiler_params=_cparams(("arbitrary", "arbitrary"), 24 << 20),
        name="neighborhood_attn",
    )(nq3, nk3, nv3, nk3, nv3, bias_tab)


def _na_bias_table(rel_bias):
    j = np.arange(GRID_W)
    col_start = np.clip(j - NA_WIN_W // 2, 0, GRID_W - NA_WIN_W)
    col_valid = (j[None, :] >= col_start[:, None]) & (j[None, :] < col_start[:, None] + NA_WIN_W)
    col_idx = np.clip(j[None, :] - j[:, None] + NA_WIN_W - 1, 0, 2 * NA_WIN_W - 2)
    ds = np.arange(NA_WIN_H)[:, None] + np.arange(NA_WIN_H)[None, :]
    tab = rel_bias[:, ds][:, :, :, col_idx]
    tab = jnp.where(col_valid[None, None, None], tab, NEG)
    tab = tab.transpose(1, 0, 3, 2, 4)
    return tab.reshape(NA_WIN_H, D_HEADS, GRID_W, NA_WIN_H * GRID_W).astype(F32)


def _router_kernel(x_ref, sh_ref, sc_ref, wh_ref, wl_ref, b_ref, tri_ref, o_ref, cnt_ref, base_ref, *, spb):
    j = pl.program_id(0)

    @pl.when(j == 0)
    def _():
        base_ref[...] = jnp.zeros_like(base_ref)

    r = 1 + j // spb
    u = x_ref[...] * (1.0 + sc_ref[pl.ds(r, 1), :]) + sh_ref[pl.ds(r, 1), :]
    u_hi = u.astype(BF16)
    u_lo = (u - u_hi.astype(F32)).astype(BF16)
    wh = wh_ref[...]
    logits = _dot(u_hi, wh) + _dot(u_lo, wh) + _dot(u_hi, wl_ref[...]) + b_ref[...]
    lane = lax.broadcasted_iota(jnp.int32, (TM, LANES), 1)
    m1 = jnp.max(logits, axis=-1, keepdims=True)
    i1 = jnp.min(jnp.where(logits == m1, lane, LANES), axis=-1, keepdims=True)
    rest = jnp.where(lane == i1, NEG * 2, logits)
    m2 = jnp.max(rest, axis=-1, keepdims=True)
    i2 = jnp.min(jnp.where(rest == m2, lane, LANES), axis=-1, keepdims=True)
    e21 = jnp.exp(m2 - m1)
    g1 = 1.0 / (1.0 + e21)
    g2 = e21 / (1.0 + e21)
    oh1 = (lane == i1).astype(F32)
    oh2 = (lane == i2).astype(F32)
    both = oh1 + oh2
    before = _dot(tri_ref[...], both.astype(BF16)) + base_ref[0:1, :]
    r1 = jnp.sum(oh1 * before, axis=-1, keepdims=True)
    r2 = jnp.sum(oh2 * before, axis=-1, keepdims=True)
    out = jnp.zeros((TM, LANES), F32)
    for idx, col in enumerate((i1.astype(F32), i2.astype(F32), g1, g2, r1, r2)):
        out = jnp.where(lane == idx, col, out)
    o_ref[...] = out
    total = base_ref[0:1, :] + jnp.sum(both, axis=0, keepdims=True)
    base_ref[...] = jnp.broadcast_to(total, base_ref.shape)
    cnt_ref[...] = jnp.broadcast_to(total, cnt_ref.shape)


def _router(x1, modt, wr_hi, wr_lo, br, tri, *, nblk, spb):
    kern = functools.partial(_router_kernel, spb=spb)
    return pl.pallas_call(
        kern,
        grid=(nblk,),
        in_specs=[pl.BlockSpec((TM, D_MODEL), lambda j: (j, 0)),
                  _mod_spec(1, 3, 1), _mod_spec(1, 4, 1),
                  pl.BlockSpec((D_MODEL, LANES), lambda j: (0, 0)),
                  pl.BlockSpec((D_MODEL, LANES), lambda j: (0, 0)),
                  pl.BlockSpec((1, LANES), lambda j: (0, 0)),
                  pl.BlockSpec((TM, TM), lambda j: (0, 0))],
        out_specs=[pl.BlockSpec((TM, LANES), lambda j: (j, 0)),
                   pl.BlockSpec((8, LANES), lambda j: (0, 0))],
        out_shape=[jax.ShapeDtypeStruct((nblk * TM, LANES), F32),
                   jax.ShapeDtypeStruct((8, LANES), F32)],
        scratch_shapes=[pltpu.VMEM((8, LANES), F32)],
        compiler_params=_cparams(("arbitrary",), 24 << 20),
        name="moe_router",
    )(x1, modt, modt, wr_hi, wr_lo, br, tri)


def _dispatch_kernel(dest_ref, x_ref, sh_ref, sc_ref, xp_in_ref, xp_ref, u_ref, sem, *, spb):
    del xp_in_ref
    j = pl.program_id(0)
    r = 1 + j // spb
    u_ref[...] = x_ref[...] * (1.0 + sc_ref[pl.ds(r, 1), :]) + sh_ref[pl.ds(r, 1), :]

    def row_copy(i, k):
        d = dest_ref[0, k * TM + i]
        return pltpu.make_async_copy(u_ref.at[pl.ds(i, 1)], xp_ref.at[pl.ds(d, 1)], sem)

    def start(i, c):
        row_copy(i, 0).start()
        row_copy(i, 1).start()
        return c

    lax.fori_loop(0, TM, start, 0)

    def wait(i, c):
        row_copy(i, 0).wait()
        row_copy(i, 1).wait()
        return c

    lax.fori_loop(0, TM, wait, 0)


def _dispatch(dest_blk, x1, modt, xpad0, *, nblk, spb):
    kern = functools.partial(_dispatch_kernel, spb=spb)
    return pl.pallas_call(
        kern,
        grid=(nblk,),
        in_specs=[pl.BlockSpec((None, 1, 2 * TM), lambda j: (j, 0, 0), memory_space=pltpu.SMEM),
                  pl.BlockSpec((TM, D_MODEL), lambda j: (j, 0)),
                  _mod_spec(1, 3, 1), _mod_spec(1, 4, 1),
                  pl.BlockSpec(memory_space=pl.ANY)],
        out_specs=pl.BlockSpec(memory_space=pl.ANY),
        out_shape=jax.ShapeDtypeStruct(xpad0.shape, F32),
        scratch_shapes=[pltpu.VMEM((TM, D_MODEL), F32), pltpu.SemaphoreType.DMA(())],
        input_output_aliases={4: 0},
        compiler_params=_cparams(("arbitrary",), 24 << 20),
        name="moe_dispatch",
    )(dest_blk, x1, modt, modt, xpad0)


def _expert_kernel(be_ref, x_ref, wg_ref, wu_ref, wd_ref, o_ref):
    del be_ref
    f = pl.program_id(1)
    xb = x_ref[...].astype(BF16)
    hg = _dot(xb, wg_ref[...])
    hu = _dot(xb, wu_ref[...])
    h = (hg * jax.nn.sigmoid(hg) * hu).astype(BF16)
    part = _dot(h, wd_ref[...])

    @pl.when(f == 0)
    def _():
        o_ref[...] = part

    @pl.when(f > 0)
    def _():
        o_ref[...] += part


def _experts(blk_expert, xpad, wg, wu, wd):
    npad = xpad.shape[0]
    nb = npad // MOE_BM
    nf = wg.shape[2] // MOE_TF
    grid_spec = pltpu.PrefetchScalarGridSpec(
        num_scalar_prefetch=1,
        grid=(nb, nf),
        in_specs=[pl.BlockSpec((MOE_BM, D_MODEL), lambda i, f, be: (i, 0)),
                  pl.BlockSpec((None, D_MODEL, MOE_TF), lambda i, f, be: (be[i], 0, f)),
                  pl.BlockSpec((None, D_MODEL, MOE_TF), lambda i, f, be: (be[i], 0, f)),
                  pl.BlockSpec((None, MOE_TF, D_MODEL), lambda i, f, be: (be[i], f, 0))],
        out_specs=pl.BlockSpec((MOE_BM, D_MODEL), lambda i, f, be: (i, 0)),
    )
    return pl.pallas_call(
        _expert_kernel,
        grid_spec=grid_spec,
        out_shape=jax.ShapeDtypeStruct((npad, D_MODEL), F32),
        compiler_params=_cparams(("arbitrary", "arbitrary"), 40 << 20),
        name="moe_experts",
    )(blk_expert, xpad, wg, wu, wd)


def _combine_kernel(dest_ref, yp_ref, x_ref, rt_ref, gate_ref, g_ref, beta_ref, o_ref, ybuf_ref, sem, *, spb):
    j = pl.program_id(0)
    r = 1 + j // spb

    def row_copy(i, k):
        d = dest_ref[0, k * TM + i]
        return pltpu.make_async_copy(yp_ref.at[pl.ds(d, 1)], ybuf_ref.at[k, pl.ds(i, 1)], sem)

    def start(i, c):
        row_copy(i, 0).start()
        row_copy(i, 1).start()
        return c

    lax.fori_loop(0, TM, start, 0)

    def wait(i, c):
        row_copy(i, 0).wait()
        row_copy(i, 1).wait()
        return c

    lax.fori_loop(0, TM, wait, 0)
    rt = rt_ref[...]
    g1 = rt[:, 2:3]
    g2 = rt[:, 3:4]
    y = g1 * ybuf_ref[0] + g2 * ybuf_ref[1]
    gate = gate_ref[pl.ds(r, 1), :]
    o_ref[...] = _layer_norm(ALPHA * x_ref[...] + gate * y, g_ref[...], beta_ref[...])


def _combine(dest_blk, ypad, x1, rt, modt, g, beta, *, nblk, spb):
    kern = functools.partial(_combine_kernel, spb=spb)
    return pl.pallas_call(
        kern,
        grid=(nblk,),
        in_specs=[pl.BlockSpec((None, 1, 2 * TM), lambda j: (j, 0, 0), memory_space=pltpu.SMEM),
                  pl.BlockSpec(memory_space=pl.ANY),
                  pl.BlockSpec((TM, D_MODEL), lambda j: (j, 0)),
                  pl.BlockSpec((TM, LANES), lambda j: (j, 0)),
                  _mod_spec(1, 5, 1),
                  pl.BlockSpec((1, D_MODEL), lambda j: (0, 0)),
                  pl.BlockSpec((1, D_MODEL), lambda j: (0, 0))],
        out_specs=pl.BlockSpec((TM, D_MODEL), lambda j: (j, 0)),
        out_shape=jax.ShapeDtypeStruct((nblk * TM, D_MODEL), F32),
        scratch_shapes=[pltpu.VMEM((2, TM, D_MODEL), F32), pltpu.SemaphoreType.DMA(())],
        compiler_params=_cparams(("arbitrary",), 24 << 20),
        name="moe_combine_ln",
    )(dest_blk, ypad, x1, rt, modt, g, beta)


def _head_perm(partner, heads, width):
    return np.concatenate([h * width + partner for h in range(heads)])


def _even_weights(w_in, q_gain, k_gain, partner):
    perm_q = _head_perm(partner, A_Q_HEADS, HEAD_DIM)
    perm_k = _head_perm(partner, A_KV_HEADS, HEAD_DIM)
    wq = w_in[:, :A_Q_W]
    wk = w_in[:, A_Q_W:A_Q_W + A_KV_W]
    w_ext = jnp.concatenate([w_in, wq[:, perm_q], wk[:, perm_k]], axis=1).astype(BF16)
    gvec = jnp.zeros((8, LANES), F32)
    gvec = gvec.at[0].set(jnp.tile(q_gain, 2)).at[1].set(jnp.tile(q_gain[partner], 2))
    gvec = gvec.at[2].set(jnp.tile(k_gain, 2)).at[3].set(jnp.tile(k_gain[partner], 2))
    return w_ext, gvec


def _block_diag(blocks):
    n = blocks.shape[0]
    w = blocks.shape[1]
    out = jnp.zeros((n * w, n * w), blocks.dtype)
    for g in range(n):
        out = out.at[g * w:(g + 1) * w, g * w:(g + 1) * w].set(blocks[g])
    return out


def _odd_weights(w_in, w_q_up, w_kv_up, partner):
    o_kr = C_Q_RANK + C_KV_RANK
    w_kr = w_in[:, o_kr:o_kr + C_ROPE]
    w_rest = jnp.concatenate([w_in[:, :o_kr], w_in[:, o_kr + C_ROPE:]], axis=1)
    kr_blk = jnp.concatenate([w_kr, w_kr[:, partner], jnp.zeros((D_MODEL, LANES - 2 * C_ROPE), F32)], axis=1)
    w_ext = jnp.concatenate([w_rest, kr_blk], axis=1).astype(BF16)
    dq = C_NOPE + C_ROPE
    wq3 = w_q_up.reshape(C_Q_RANK, C_HEADS, dq)
    zq = jnp.zeros((C_Q_RANK, C_HEADS, LANES - dq), F32)
    wq_pad = jnp.concatenate([wq3, zq], axis=-1).reshape(C_Q_RANK, C_HEADS * LANES)
    wq_rope_p = wq3[:, :, C_NOPE:][:, :, partner]
    wq_padp = jnp.concatenate([jnp.zeros((C_Q_RANK, C_HEADS, C_NOPE), F32), wq_rope_p, zq], axis=-1)
    wq_padp = wq_padp.reshape(C_Q_RANK, C_HEADS * LANES)
    wkv3 = w_kv_up.reshape(C_KV_RANK, C_HEADS, C_NOPE + C_V)
    wk_pad = jnp.concatenate([wkv3[:, :, :C_NOPE], jnp.zeros((C_KV_RANK, C_HEADS, LANES - C_NOPE), F32)], axis=-1)
    wk_pad = wk_pad.reshape(C_KV_RANK, C_HEADS * LANES)
    wv = wkv3[:, :, C_NOPE:].reshape(C_KV_RANK, C_HEADS * C_V)
    pl2 = np.zeros((LANES, C_HEADS * LANES), np.float32)
    for h in range(C_HEADS):
        for dd in range(C_ROPE):
            pl2[dd, h * LANES + C_NOPE + dd] = 1.0
            pl2[C_ROPE + dd, h * LANES + C_NOPE + dd] = 1.0
    return (w_ext, wq_pad.astype(BF16), wq_padp.astype(BF16), wk_pad.astype(BF16), wv.astype(BF16),
            jnp.asarray(pl2, BF16))


def kernel(x, c, ctx, c_ctx, w_ada, b_ada, ln1_g, ln1_b, ln2_g, ln2_b,
           ev_w_in, ev_w_out, ev_q_gain, ev_k_gain, ev_w_pool, ev_pool_scale,
           ev_w_gate, ev_w_up, ev_w_down,
           od_w_in, od_w_out, od_q_lat_gain, od_kv_lat_gain, od_w_q_up, od_w_kv_up, od_na_bias,
           od_w_router, od_b_router, od_w_gate, od_w_up, od_w_down):
    batch, seq, d = x.shape
    ctx_len = ctx.shape[1]
    assert d == D_MODEL and ctx_len == TM and seq % TM == 0 and seq % GRID_W == 0
    assert seq // GRID_W >= NA_WIN_H and batch + 1 <= 8
    nt = ctx_len + seq
    nbpb = nt // TM
    nblk = batch * nbpb
    spb = seq // TM
    nxb = batch * spb

    xa = jnp.concatenate([ctx, x], axis=1).reshape(batch * nt, d)
    cin = jnp.zeros((8, d), F32).at[0].set(c_ctx).at[1:1 + batch].set(c)
    modt = _ada_table(cin, w_ada, b_ada)

    cos64, sin64, partner64 = _rope_axis_tables(HEAD_DIM, seq, ctx_len)
    w_ext0, gvec = _even_weights(ev_w_in[0], ev_q_gain[0], ev_k_gain[0], partner64)
    cos2 = jnp.tile(cos64, (1, 2))
    sin2 = jnp.tile(sin64, (1, 2))
    ones_bd = _block_diag(jnp.ones((2, HEAD_DIM, HEAD_DIM), BF16))
    qt, kk, vt, pp = _even_inproj(xa, modt, w_ext0, cos2, sin2, gvec, ones_bd, batch=batch, nt=nt)
    attn = _flash(qt, kk.reshape(batch, nt, A_KV_W), vt, batch=batch, nt=nt, units=A_KV_HEADS // 2,
                  nq=2 * A_GROUP, kw=LANES, k_off=(0,) * (2 * A_GROUP),
                  v_off=tuple((jq // A_GROUP) * HEAD_DIM for jq in range(2 * A_GROUP)),
                  ctx_queries=True, name="gqa_flash")
    pooled = _pool(pp, _block_diag(ev_w_pool[0]).astype(BF16), ev_pool_scale[0].reshape(1, POOL_WIDTH),
                   batch=batch, nt=nt)
    w_out0 = ev_w_out[0].astype(BF16)
    x1 = _proj_ln(attn.reshape(batch * nt, A_Q_W), pooled, w_out0[:A_Q_W], w_out0[A_Q_W:], xa, modt,
                  ln1_g[0].reshape(1, d), ln1_b[0].reshape(1, d), layer=0, nblk=nblk,
                  x_map=lambda j: (j, 0), row_fn=lambda j: _mod_row_all(j, nbpb), name="even_out_ln")
    xa1 = _ffn_ln(x1, modt, ev_w_gate[0].astype(BF16), ev_w_up[0].astype(BF16), ev_w_down[0].astype(BF16),
                  ln2_g[0].reshape(1, d), ln2_b[0].reshape(1, d), nbpb=nbpb, nblk=nblk, tm=TM, tf=D_FF // 2)

    cos32, sin32, partner32 = _rope_axis_tables(C_ROPE, seq, ctx_len)
    w_ext1, wq_pad, wq_padp, wk_pad, wv, pl2 = _odd_weights(od_w_in[0], od_w_q_up[0], od_w_kv_up[0], partner32)
    ntr = cos32.shape[0]
    pad_tail = jnp.zeros((ntr, LANES - C_NOPE - C_ROPE), F32)
    cos_q = jnp.tile(jnp.concatenate([jnp.ones((ntr, C_NOPE), F32), cos32, pad_tail], axis=1), (1, C_HEADS))
    sin_q = jnp.tile(jnp.concatenate([jnp.zeros((ntr, C_NOPE), F32), sin32, pad_tail], axis=1), (1, C_HEADS))
    cs_k = jnp.concatenate([cos32, sin32, jnp.zeros((ntr, LANES - 2 * C_ROPE), F32)], axis=1)
    qt1, k1, vt1, nq, nk, nv = _odd_inproj(
        xa1, modt, w_ext1, od_q_lat_gain[0].reshape(1, C_Q_RANK), od_kv_lat_gain[0].reshape(1, C_KV_RANK),
        wq_pad, wq_padp, wk_pad, wv, pl2, cos_q, sin_q, cs_k, batch=batch, nt=nt)
    mla = _flash(qt1, k1.reshape(batch, nt, C_HEADS * LANES), vt1, batch=batch, nt=nt, units=C_HEADS // 2,
                 nq=2, kw=2 * LANES, k_off=(0, LANES), v_off=(0, C_V), ctx_queries=False, name="mla_flash")
    na = _na(nq, nk, nv, _na_bias_table(od_na_bias[0]), batch=batch, nt=nt, seq=seq)
    w_out1 = od_w_out[0].astype(BF16)
    x1b = _proj_ln(mla.reshape(batch * seq, C_HEADS * C_V), na.reshape(batch * seq, NA_W),
                   w_out1[:C_HEADS * C_V], w_out1[C_HEADS * C_V:], xa1, modt,
                   ln1_g[1].reshape(1, d), ln1_b[1].reshape(1, d), layer=1, nblk=nxb,
                   x_map=lambda j: (j + j // spb + 1, 0), row_fn=lambda j: 1 + j // spb, name="odd_out_ln")

    wr = jnp.zeros((d, LANES), F32).at[:, :N_EXPERTS].set(od_w_router[0])
    wr_hi = wr.astype(BF16)
    wr_lo = (wr - wr_hi.astype(F32)).astype(BF16)
    br = jnp.full((1, LANES), NEG, F32).at[0, :N_EXPERTS].set(od_b_router[0])
    tri = jnp.asarray(np.tril(np.ones((TM, TM), np.float32), -1), BF16)
    rt, cnt = _router(x1b, modt, wr_hi, wr_lo, br, tri, nblk=nxb, spb=spb)

    t_tok = batch * seq
    counts = cnt[0, :N_EXPERTS].astype(jnp.int32)
    padded = (counts + MOE_BM - 1) // MOE_BM * MOE_BM
    pad_end = jnp.cumsum(padded)
    pad_start = pad_end - padded
    n_blocks = -(-2 * t_tok // MOE_BM) + N_EXPERTS
    e12 = rt[:, 0:2].astype(jnp.int32)
    r12 = rt[:, 4:6].astype(jnp.int32)
    dest = pad_start[e12] + r12
    dest_blk = dest.reshape(nxb, TM, 2).transpose(0, 2, 1).reshape(nxb, 1, 2 * TM)
    blk_start = jnp.arange(n_blocks, dtype=jnp.int32) * MOE_BM
    blk_expert = jnp.minimum(jnp.sum((pad_end[None, :] <= blk_start[:, None]).astype(jnp.int32), axis=1),
                             N_EXPERTS - 1)

    xpad = _dispatch(dest_blk, x1b, modt, jnp.zeros((n_blocks * MOE_BM, d), F32), nblk=nxb, spb=spb)
    ypad = _experts(blk_expert, xpad, od_w_gate[0].astype(BF16), od_w_up[0].astype(BF16),
                    od_w_down[0].astype(BF16))
    out = _combine(dest_blk, ypad, x1b, rt, modt, ln2_g[1].reshape(1, d), ln2_b[1].reshape(1, d),
                   nblk=nxb, spb=spb)
    return out.reshape(batch, seq, d)
```

```python
import functools
import math

import numpy as np
import jax
import jax.numpy as jnp
from jax import lax
from jax.experimental import pallas as pl
from jax.experimental.pallas import tpu as pltpu

F32 = jnp.float32
BF16 = jnp.bfloat16

D_MODEL = 1024
DEPTH = 2
GRID_W = 64
ROPE_THETA = 10000.0
NORM_EPS = 1e-6
ALPHA = (2 * DEPTH) ** 0.25
HEAD_DIM = 64
A_Q_HEADS = 12
A_KV_HEADS = 4
A_GROUP = A_Q_HEADS // A_KV_HEADS
POOL_WINDOWS = (2, 4, 8, 16)
POOL_GROUP = 64
POOL_WIDTH = POOL_GROUP * len(POOL_WINDOWS)
C_HEADS = 8
C_Q_RANK = 384
C_KV_RANK = 256
C_NOPE = 64
C_ROPE = 32
C_V = 64
D_HEADS = 8
NA_WIN_H = 8
NA_WIN_W = 16
D_FF = 2816
N_EXPERTS = 8
D_FF_EXPERT = 3584
A_Q_W = A_Q_HEADS * HEAD_DIM
A_KV_W = A_KV_HEADS * HEAD_DIM
NA_W = D_HEADS * HEAD_DIM

LANES = 128
TM = 256
VMEM_CAP = 56 * 1024 * 1024
MOE_BM = 512
MOE_TF = 1792
VROWS = HEAD_DIM + 16
FLASH_STAGES = 24
FLASH_LOOKAHEAD = 6
NEG = -1e30
LOG2E = 1.4426950408889634


def _cparams(sem, vmem_bytes):
    return pltpu.CompilerParams(dimension_semantics=sem,
                                vmem_limit_bytes=int(min(VMEM_CAP, max(vmem_bytes, 16 * 1024 * 1024))))


def _dot(a, b):
    return jnp.dot(a, b, preferred_element_type=F32)


def _dot_nt(a, b):
    return lax.dot_general(a, b, (((1,), (1,)), ((), ())), preferred_element_type=F32)


def _layer_norm(z, g, b):
    mu = jnp.mean(z, axis=-1, keepdims=True)
    zc = z - mu
    var = jnp.mean(zc * zc, axis=-1, keepdims=True)
    return zc * lax.rsqrt(var + NORM_EPS) * g + b


def _ada_kernel(c_ref, w_ref, b_ref, o_ref):
    c = c_ref[...]
    s = c * jax.nn.sigmoid(c)
    o_ref[...] = _dot(s.astype(BF16), w_ref[...].astype(BF16)) + b_ref[...]


def _ada_table(cin, w_ada, b_ada):
    depth, d, n = w_ada.shape
    return pl.pallas_call(
        _ada_kernel,
        grid=(depth, n // d),
        in_specs=[pl.BlockSpec((8, d), lambda l, j: (0, 0)),
                  pl.BlockSpec((None, d, d), lambda l, j: (l, 0, j)),
                  pl.BlockSpec((None, 1, d), lambda l, j: (l, 0, j))],
        out_specs=pl.BlockSpec((None, 8, d), lambda l, j: (l, 0, j)),
        out_shape=jax.ShapeDtypeStruct((depth, 8, n), F32),
        compiler_params=_cparams(("arbitrary", "arbitrary"), 24 << 20),
        name="ada_table",
    )(cin, w_ada, b_ada.reshape(depth, 1, n))


def _mod_spec(layer, k, nargs):
    if nargs == 1:
        return pl.BlockSpec((None, 8, D_MODEL), lambda j: (layer, 0, k))
    return pl.BlockSpec((None, 8, D_MODEL), lambda *_: (layer, 0, k))


def _mod_row_all(j, nbpb):
    return jnp.where(j % nbpb == 0, 0, 1 + j // nbpb)


def _rope_axis_tables(rope_dims, seq, ctx):
    n = rope_dims // 2
    half = n // 2
    t = jnp.arange(seq, dtype=jnp.int32)
    row = (t // GRID_W).astype(F32)
    col = (t % GRID_W).astype(F32)
    inv = ROPE_THETA ** (-jnp.arange(half, dtype=F32) / half)
    cols_c, cols_s = [], []
    for pos in (row, col):
        ang = pos[:, None] * inv[None, :]
        c, s = jnp.cos(ang), jnp.sin(ang)
        cols_c += [c, c]
        cols_s += [-s, s]
    cos = jnp.concatenate(cols_c, axis=-1)
    sin = jnp.concatenate(cols_s, axis=-1)
    cos = jnp.concatenate([jnp.ones((ctx, rope_dims), F32), cos], axis=0)
    sin = jnp.concatenate([jnp.zeros((ctx, rope_dims), F32), sin], axis=0)
    d = np.arange(rope_dims)
    partner = (d // n) * n + ((d % n) + half) % n
    return cos, sin, partner


def _store_vt(vt_ref, vt, heads):
    ones = jnp.ones((VROWS - HEAD_DIM, TM), BF16)
    for h in range(heads):
        vt_ref[h * VROWS:h * VROWS + HEAD_DIM, :] = vt[h * HEAD_DIM:(h + 1) * HEAD_DIM, :]
        vt_ref[h * VROWS + HEAD_DIM:(h + 1) * VROWS, :] = ones


def _even_inproj_kernel(x_ref, sh_ref, sc_ref, w_ref, cos_ref, sin_ref, gv_ref, ones_ref,
                        qt_ref, k_ref, vt_ref, p_ref, *, nbpb):
    j = pl.program_id(0)
    r = _mod_row_all(j, nbpb)
    shift = sh_ref[pl.ds(r, 1), :]
    scale = sc_ref[pl.ds(r, 1), :]
    u = x_ref[...] * (1.0 + scale) + shift
    acc = _dot(u.astype(BF16), w_ref[...])
    cos = cos_ref[...]
    sin = sin_ref[...]
    ones_bd = ones_ref[...]

    def norm_rope(a, ap, g, gp):
        ssq = _dot((a * a).astype(BF16), ones_bd)
        rinv = lax.rsqrt(ssq * (1.0 / HEAD_DIM) + NORM_EPS)
        return (a * g * cos + ap * gp * sin) * rinv

    gq, gqp, gk, gkp = gv_ref[0:1, :], gv_ref[1:2, :], gv_ref[2:3, :], gv_ref[3:4, :]
    p_off = A_Q_W + 2 * A_KV_W + POOL_WIDTH
    zeros_half = jnp.zeros((HEAD_DIM, TM), BF16)
    for c in range(A_Q_W // LANES):
        a = acc[:, c * LANES:(c + 1) * LANES]
        ap = acc[:, p_off + c * LANES:p_off + (c + 1) * LANES]
        rot = norm_rope(a, ap, gq, gqp) * (HEAD_DIM ** -0.5 * LOG2E)
        rot_t = rot.T
        for hl in range(2):
            head = 2 * c + hl
            slot = (head // A_GROUP) % 2
            base = head * LANES
            qt_ref[base + slot * HEAD_DIM:base + (slot + 1) * HEAD_DIM, :] = (
                rot_t[hl * HEAD_DIM:(hl + 1) * HEAD_DIM, :].astype(BF16))
            qt_ref[base + (1 - slot) * HEAD_DIM:base + (2 - slot) * HEAD_DIM, :] = zeros_half
    for c in range(A_KV_W // LANES):
        a = acc[:, A_Q_W + c * LANES:A_Q_W + (c + 1) * LANES]
        ap = acc[:, p_off + A_Q_W + c * LANES:p_off + A_Q_W + (c + 1) * LANES]
        k_ref[:, c * LANES:(c + 1) * LANES] = norm_rope(a, ap, gk, gkp).astype(BF16)
    v = acc[:, A_Q_W + A_KV_W:A_Q_W + 2 * A_KV_W]
    _store_vt(vt_ref, v.T.astype(BF16), A_KV_HEADS)
    p_ref[...] = acc[:, A_Q_W + 2 * A_KV_W:A_Q_W + 2 * A_KV_W + POOL_WIDTH]


def _even_inproj(xa, modt, w_ext, cos2, sin2, gvec, ones_bd, *, batch, nt):
    nbpb = nt // TM
    nblk = batch * nbpb
    n_ext = w_ext.shape[1]
    kern = functools.partial(_even_inproj_kernel, nbpb=nbpb)
    return pl.pallas_call(
        kern,
        grid=(nblk,),
        in_specs=[pl.BlockSpec((TM, D_MODEL), lambda j: (j, 0)),
                  _mod_spec(0, 0, 1), _mod_spec(0, 1, 1),
                  pl.BlockSpec((D_MODEL, n_ext), lambda j: (0, 0)),
                  pl.BlockSpec((TM, LANES), lambda j: (j % nbpb, 0)),
                  pl.BlockSpec((TM, LANES), lambda j: (j % nbpb, 0)),
                  pl.BlockSpec((8, LANES), lambda j: (0, 0)),
                  pl.BlockSpec((LANES, LANES), lambda j: (0, 0))],
        out_specs=[pl.BlockSpec((None, A_Q_HEADS * LANES, TM), lambda j: (j // nbpb, 0, j % nbpb)),
                   pl.BlockSpec((TM, A_KV_W), lambda j: (j, 0)),
                   pl.BlockSpec((None, None, A_KV_HEADS * VROWS, TM), lambda j: (j // nbpb, j % nbpb, 0, 0)),
                   pl.BlockSpec((TM, POOL_WIDTH), lambda j: (j, 0))],
        out_shape=[jax.ShapeDtypeStruct((batch, A_Q_HEADS * LANES, nt), BF16),
                   jax.ShapeDtypeStruct((batch * nt, A_KV_W), BF16),
                   jax.ShapeDtypeStruct((batch, nbpb, A_KV_HEADS * VROWS, TM), BF16),
                   jax.ShapeDtypeStruct((batch * nt, POOL_WIDTH), F32)],
        compiler_params=_cparams(("arbitrary",), 40 << 20),
        name="even_inproj",
    )(xa, modt, modt, w_ext, cos2, sin2, gvec, ones_bd)


def _flash_kernel(q_ref, k_ref, v_ref, o_ref, acc_ref, m_ref, out_ref, *, nq, k_off, v_off, ctx_queries, nsub):
    nch = v_ref.shape[0]
    acc_ref[...] = jnp.zeros_like(acc_ref)
    m_ref[...] = jnp.full_like(m_ref, NEG)

    def scores(j, c):
        start = pl.multiple_of(c * TM, TM)
        kc = k_ref[pl.ds(start, TM), k_off[j]:k_off[j] + LANES]
        return _dot(kc, q_ref[j * LANES:(j + 1) * LANES, :])

    def update(j, c, s):
        row = slice(j, j + 1)
        m_old = m_ref[row, :]
        m_new = jnp.maximum(m_old, jnp.max(s, axis=0, keepdims=True))
        alpha = jnp.exp2(m_old - m_new)
        p = jnp.exp2(s - m_new)
        m_ref[row, :] = m_new
        pv = _dot(v_ref[c, v_off[j]:v_off[j] + VROWS, :], p.astype(BF16))
        rows = slice(j * VROWS, (j + 1) * VROWS)
        acc_ref[rows, :] = alpha * acc_ref[rows, :] + pv

    def run(chunks):
        stages = [(j, c) for c in chunks for j in range(nq)]
        ahead = [scores(*st) for st in stages[:FLASH_LOOKAHEAD]]
        for i, (j, c) in enumerate(stages):
            if i + FLASH_LOOKAHEAD < len(stages):
                ahead.append(scores(*stages[i + FLASH_LOOKAHEAD]))
            update(j, c, ahead.pop(0))

    run([0])
    n_main = (nch - 1) // nsub
    if ctx_queries:
        n_main = jnp.where(pl.program_id(2) == 0, 0, n_main)

    def body(it, carry):
        run([1 + it * nsub + t for t in range(nsub)])
        return carry

    lax.fori_loop(0, n_main, body, 0)
    for j in range(nq):
        denom = acc_ref[j * VROWS + HEAD_DIM:j * VROWS + HEAD_DIM + 1, :]
        out_ref[j * HEAD_DIM:(j + 1) * HEAD_DIM, :] = acc_ref[j * VROWS:j * VROWS + HEAD_DIM, :] / denom
    o_ref[...] = out_ref[...].T.astype(BF16)


def _flash(qt, k, vt, *, batch, nt, units, nq, kw, k_off, v_off, ctx_queries, name):
    nbpb = nt // TM
    nqb = nbpb if ctx_queries else nbpb - 1
    qoff = 0 if ctx_queries else 1
    nv = vt.shape[2] // units
    nsub = max(d for d in range(1, FLASH_STAGES // nq + 1) if (nbpb - 1) % d == 0)
    kern = functools.partial(_flash_kernel, nq=nq, k_off=k_off, v_off=v_off, ctx_queries=ctx_queries, nsub=nsub)
    vmem = 2 * (nt * kw * 2 + nt * nv * 2) + 4 * nq * LANES * TM * 2 + 3 * nq * HEAD_DIM * TM * 4 + (8 << 20)
    return pl.pallas_call(
        kern,
        grid=(batch, units, nqb),
        in_specs=[pl.BlockSpec((None, nq * LANES, TM), lambda b, u, i: (b, u, i + qoff)),
                  pl.BlockSpec((None, nt, kw), lambda b, u, i: (b, 0, u)),
                  pl.BlockSpec((None, nbpb, nv, TM), lambda b, u, i: (b, 0, u, 0))],
        out_specs=pl.BlockSpec((None, TM, nq * HEAD_DIM), lambda b, u, i: (b, i, u)),
        out_shape=jax.ShapeDtypeStruct((batch, nqb * TM, units * nq * HEAD_DIM), BF16),
        scratch_shapes=[pltpu.VMEM((nq * VROWS, TM), F32),
                        pltpu.VMEM((8, TM), F32),
                        pltpu.VMEM((nq * HEAD_DIM, TM), F32)],
        compiler_params=_cparams(("arbitrary", "arbitrary", "arbitrary"), vmem),
        name=name,
    )(qt, k, vt)


def _pool_kernel(prev_ref, cur_ref, next_ref, w_ref, ps_ref, o_ref, *, nbpb, nt):
    jj = pl.program_id(0) % nbpb
    base = jj * TM
    seg_lo = jnp.where(jj == 0, 0, TM)
    seg_hi = jnp.where(jj == 0, TM, nt)
    cur = cur_ref[...]
    xcat = jnp.concatenate([prev_ref[...], cur, next_ref[...]], axis=0)
    x_hi = xcat.astype(BF16)
    x_lo = (xcat - x_hi.astype(F32)).astype(BF16)
    t = base + lax.broadcasted_iota(jnp.int32, (TM, 3 * TM), 0)
    s = base - TM + lax.broadcasted_iota(jnp.int32, (TM, 3 * TM), 1)
    t1 = base + lax.broadcasted_iota(jnp.int32, (TM, 1), 0)
    lane_grp = lax.broadcasted_iota(jnp.int32, (TM, POOL_WIDTH), 1) // POOL_GROUP
    mean = jnp.zeros((TM, POOL_WIDTH), F32)
    for g, win in enumerate(POOL_WINDOWS):
        lo = win // 2
        hi = win - lo
        valid = ((s >= jnp.maximum(t - lo, seg_lo)) & (s < jnp.minimum(t + hi, seg_hi)))
        vb = valid.astype(F32).astype(BF16)
        wsum = _dot(vb, x_hi) + _dot(vb, x_lo)
        cnt = (jnp.minimum(t1 + hi, seg_hi) - jnp.maximum(t1 - lo, seg_lo)).astype(F32)
        mean = jnp.where(lane_grp == g, wsum / cnt, mean)
    y = (mean - cur).astype(BF16)
    o_ref[...] = (_dot(y, w_ref[...]) * ps_ref[...]).astype(BF16)


def _pool(p, w_bd, ps, *, batch, nt):
    nbpb = nt // TM
    nblk = batch * nbpb
    kern = functools.partial(_pool_kernel, nbpb=nbpb, nt=nt)

    def prev_map(j):
        return (jnp.maximum(j - 1, 0), 0)

    def next_map(j):
        return (jnp.minimum(j + 1, nblk - 1), 0)

    return pl.pallas_call(
        kern,
        grid=(nblk,),
        in_specs=[pl.BlockSpec((TM, POOL_WIDTH), prev_map),
                  pl.BlockSpec((TM, POOL_WIDTH), lambda j: (j, 0)),
                  pl.BlockSpec((TM, POOL_WIDTH), next_map),
                  pl.BlockSpec((POOL_WIDTH, POOL_WIDTH), lambda j: (0, 0)),
                  pl.BlockSpec((1, POOL_WIDTH), lambda j: (0, 0))],
        out_specs=pl.BlockSpec((TM, POOL_WIDTH), lambda j: (j, 0)),
        out_shape=jax.ShapeDtypeStruct((batch * nt, POOL_WIDTH), BF16),
        compiler_params=_cparams(("arbitrary",), 24 << 20),
        name="pool_mixer",
    )(p, p, p, w_bd, ps)


def _proj_ln_kernel(a_ref, b_ref, wa_ref, wb_ref, x_ref, gate_ref, g_ref, beta_ref, o_ref, *, row_fn):
    r = row_fn(pl.program_id(0))
    gate = gate_ref[pl.ds(r, 1), :]
    y = _dot(a_ref[...], wa_ref[...]) + _dot(b_ref[...], wb_ref[...])
    o_ref[...] = _layer_norm(ALPHA * x_ref[...] + gate * y, g_ref[...], beta_ref[...])


def _proj_ln(a, b, wa, wb, x, modt, g, beta, *, layer, nblk, x_map, row_fn, name):
    ka, kb = a.shape[1], b.shape[1]
    kern = functools.partial(_proj_ln_kernel, row_fn=row_fn)
    return pl.pallas_call(
        kern,
        grid=(nblk,),
        in_specs=[pl.BlockSpec((TM, ka), lambda j: (j, 0)),
                  pl.BlockSpec((TM, kb), lambda j: (j, 0)),
                  pl.BlockSpec((ka, D_MODEL), lambda j: (0, 0)),
                  pl.BlockSpec((kb, D_MODEL), lambda j: (0, 0)),
                  pl.BlockSpec((TM, D_MODEL), x_map),
                  _mod_spec(layer, 2, 1),
                  pl.BlockSpec((1, D_MODEL), lambda j: (0, 0)),
                  pl.BlockSpec((1, D_MODEL), lambda j: (0, 0))],
        out_specs=pl.BlockSpec((TM, D_MODEL), lambda j: (j, 0)),
        out_shape=jax.ShapeDtypeStruct((nblk * TM, D_MODEL), F32),
        compiler_params=_cparams(("arbitrary",), 32 << 20),
        name=name,
    )(a, b, wa, wb, x, modt, g, beta)


def _ffn_ln_kernel(x_ref, sh_ref, sc_ref, gate_ref, wg_ref, wu_ref, wd_ref, g_ref, beta_ref, o_ref, *, nbpb):
    r = _mod_row_all(pl.program_id(0), nbpb)
    x = x_ref[...]
    u = (x * (1.0 + sc_ref[pl.ds(r, 1), :]) + sh_ref[pl.ds(r, 1), :]).astype(BF16)
    hg = _dot(u, wg_ref[...])
    hu = _dot(u, wu_ref[...])
    h = (hg * jax.nn.sigmoid(hg) * hu).astype(BF16)
    y = _dot(h, wd_ref[...])
    gate = gate_ref[pl.ds(r, 1), :]
    o_ref[...] = _layer_norm(ALPHA * x + gate * y, g_ref[...], beta_ref[...])


def _resident(shape):
    return pl.BlockSpec(shape, lambda *_: (0,) * len(shape), pipeline_mode=pl.Buffered(1))


def _ffn_ln(x1, modt, wg, wu, wd, g, beta, *, nbpb, nblk):
    dff = wg.shape[1]
    kern = functools.partial(_ffn_ln_kernel, nbpb=nbpb)
    return pl.pallas_call(
        kern,
        grid=(nblk,),
        in_specs=[pl.BlockSpec((TM, D_MODEL), lambda j: (j, 0)),
                  _mod_spec(0, 3, 1), _mod_spec(0, 4, 1), _mod_spec(0, 5, 1),
                  _resident((D_MODEL, dff)), _resident((D_MODEL, dff)), _resident((dff, D_MODEL)),
                  pl.BlockSpec((1, D_MODEL), lambda j: (0, 0)),
                  pl.BlockSpec((1, D_MODEL), lambda j: (0, 0))],
        out_specs=pl.BlockSpec((TM, D_MODEL), lambda j: (j, 0)),
        out_shape=jax.ShapeDtypeStruct((nblk * TM, D_MODEL), F32),
        compiler_params=_cparams(("arbitrary",), 44 << 20),
        name="even_ffn_ln",
    )(x1, modt, modt, modt, wg, wu, wd, g, beta)


def _odd_inproj_kernel(x_ref, sh_ref, sc_ref, w_ref, gq_ref, gkv_ref, wq_ref, wqp_ref, wk_ref, wv_ref,
                       pl2_ref, cq_ref, sq_ref, cs_ref,
                       qt_ref, k_ref, vt_ref, nq_ref, nk_ref, nv_ref, *, nbpb):
    j = pl.program_id(0)
    r = _mod_row_all(j, nbpb)
    u = x_ref[...] * (1.0 + sc_ref[pl.ds(r, 1), :]) + sh_ref[pl.ds(r, 1), :]
    acc = _dot(u.astype(BF16), w_ref[...])
    o = 0
    cq = acc[:, o:o + C_Q_RANK]; o += C_Q_RANK
    ckv = acc[:, o:o + C_KV_RANK]; o += C_KV_RANK
    nq = acc[:, o:o + NA_W]; o += NA_W
    nk = acc[:, o:o + NA_W]; o += NA_W
    nv = acc[:, o:o + NA_W]; o += NA_W
    krc = acc[:, o:o + LANES]

    def rms(a, g):
        return (a * lax.rsqrt(jnp.mean(a * a, axis=-1, keepdims=True) + NORM_EPS) * g).astype(BF16)

    cqn = rms(cq, gq_ref[...])
    kvn = rms(ckv, gkv_ref[...])
    cos_q = jnp.tile(cq_ref[...], (1, C_HEADS))
    sin_q = jnp.tile(sq_ref[...], (1, C_HEADS))
    q = _dot(cqn, wq_ref[...]) * cos_q + _dot(cqn, wqp_ref[...]) * sin_q
    q = q * ((C_NOPE + C_ROPE) ** -0.5 * LOG2E)
    qt_ref[...] = q.T.astype(BF16)
    kr_terms = (krc * cs_ref[...]).astype(BF16)
    k_ref[...] = (_dot(kvn, wk_ref[...]) + _dot(kr_terms, pl2_ref[...])).astype(BF16)
    _store_vt(vt_ref, _dot(kvn, wv_ref[...]).T.astype(BF16), C_HEADS)
    nq_ref[...] = (nq * HEAD_DIM ** -0.5).astype(BF16)
    nk_ref[...] = nk.astype(BF16)
    nv_ref[...] = nv.astype(BF16)


def _odd_inproj(xa, modt, w_ext, gq, gkv, wq_pad, wq_padp, wk_pad, wv, pl2, cos_q, sin_q, cs_k, *, batch, nt):
    nbpb = nt // TM
    nblk = batch * nbpb
    n_ext = w_ext.shape[1]
    hq = C_HEADS * LANES
    hv = C_HEADS * C_V
    kern = functools.partial(_odd_inproj_kernel, nbpb=nbpb)
    const = lambda j: (0, 0)
    tok = lambda j: (j, 0)
    pos = lambda j: (j % nbpb, 0)
    return pl.pallas_call(
        kern,
        grid=(nblk,),
        in_specs=[pl.BlockSpec((TM, D_MODEL), tok),
                  _mod_spec(1, 0, 1), _mod_spec(1, 1, 1),
                  pl.BlockSpec((D_MODEL, n_ext), const),
                  pl.BlockSpec((1, C_Q_RANK), const),
                  pl.BlockSpec((1, C_KV_RANK), const),
                  pl.BlockSpec((C_Q_RANK, hq), const),
                  pl.BlockSpec((C_Q_RANK, hq), const),
                  pl.BlockSpec((C_KV_RANK, hq), const),
                  pl.BlockSpec((C_KV_RANK, hv), const),
                  pl.BlockSpec((LANES, hq), const),
                  pl.BlockSpec((TM, LANES), pos),
                  pl.BlockSpec((TM, LANES), pos),
                  pl.BlockSpec((TM, LANES), pos)],
        out_specs=[pl.BlockSpec((None, hq, TM), lambda j: (j // nbpb, 0, j % nbpb)),
                   pl.BlockSpec((TM, hq), tok),
                   pl.BlockSpec((None, None, C_HEADS * VROWS, TM), lambda j: (j // nbpb, j % nbpb, 0, 0)),
                   pl.BlockSpec((TM, NA_W), tok),
                   pl.BlockSpec((TM, NA_W), tok),
                   pl.BlockSpec((TM, NA_W), tok)],
        out_shape=[jax.ShapeDtypeStruct((batch, hq, nt), BF16),
                   jax.ShapeDtypeStruct((batch * nt, hq), BF16),
                   jax.ShapeDtypeStruct((batch, nbpb, C_HEADS * VROWS, TM), BF16),
                   jax.ShapeDtypeStruct((batch * nt, NA_W), BF16),
                   jax.ShapeDtypeStruct((batch * nt, NA_W), BF16),
                   jax.ShapeDtypeStruct((batch * nt, NA_W), BF16)],
        compiler_params=_cparams(("arbitrary",), 48 << 20),
        name="odd_inproj",
    )(xa, modt, modt, w_ext, gq, gkv, wq_pad, wq_padp, wk_pad, wv, pl2, cos_q, sin_q, cs_k)


def _na_kernel(q_ref, kb_ref, vb_ref, kc_ref, vc_ref, bias_ref, o_ref):
    lane = lax.broadcasted_iota(jnp.int32, (GRID_W, LANES), 1)
    first_half = (lane // HEAD_DIM) == 0
    scores = []
    for h in range(D_HEADS):
        cols = slice((h // 2) * LANES, (h // 2 + 1) * LANES)
        qp = q_ref[:, cols]
        qh = jnp.where(first_half == (h % 2 == 0), qp, jnp.zeros_like(qp))
        s_w = _dot_nt(qh, kb_ref[0, :, cols]) + bias_ref[h]
        s_c = _dot_nt(qh, kc_ref[:, cols])
        scores.append((s_w, s_c))
    outs = []
    for h in range(D_HEADS):
        cols = slice((h // 2) * LANES, (h // 2 + 1) * LANES)
        s_w, s_c = scores[h]
        m = jnp.maximum(jnp.max(s_w, axis=-1, keepdims=True), jnp.max(s_c, axis=-1, keepdims=True))
        p_w = jnp.exp(s_w - m)
        p_c = jnp.exp(s_c - m)
        l = jnp.sum(p_w, axis=-1, keepdims=True) + jnp.sum(p_c, axis=-1, keepdims=True)
        o = _dot(p_w.astype(BF16), vb_ref[0, :, cols]) + _dot(p_c.astype(BF16), vc_ref[:, cols])
        outs.append(o / l)
    for hp in range(D_HEADS // 2):
        o_ref[:, hp * LANES:(hp + 1) * LANES] = jnp.where(first_half, outs[2 * hp], outs[2 * hp + 1]).astype(BF16)


def _na(nq, nk, nv, bias_tab, *, batch, nt, seq):
    rows = seq // GRID_W
    kh = NA_WIN_H
    band = kh * GRID_W
    ctx_rows = TM // GRID_W

    def rs_of(r):
        return jnp.clip(r - kh // 2, 0, rows - kh)

    nq3 = nq.reshape(batch, nt, NA_W)
    nk3 = nk.reshape(batch, nt, NA_W)
    nv3 = nv.reshape(batch, nt, NA_W)
    band_spec = pl.BlockSpec((pl.Element(1), pl.Element(band), pl.Element(NA_W)),
                             lambda b, r: (b, pl.multiple_of(TM + rs_of(r) * GRID_W, GRID_W), 0))
    ctx_spec = pl.BlockSpec((None, TM, NA_W), lambda b, r: (b, 0, 0))
    return pl.pallas_call(
        _na_kernel,
        grid=(batch, rows),
        in_specs=[pl.BlockSpec((None, GRID_W, NA_W), lambda b, r: (b, ctx_rows + r, 0)),
                  band_spec, band_spec, ctx_spec, ctx_spec,
                  pl.BlockSpec((None, D_HEADS, GRID_W, band), lambda b, r: (rs_of(r) - r + NA_WIN_H - 1, 0, 0, 0))],
        out_specs=pl.BlockSpec((None, GRID_W, NA_W), lambda b, r: (b, r, 0)),
        out_shape=jax.ShapeDtypeStruct((batch, seq, NA_W), BF16),
        compiler_params=_cparams(("arbitrary", "arbitrary"), 24 << 20),
        name="neighborhood_attn",
    )(nq3, nk3, nv3, nk3, nv3, bias_tab)


def _na_bias_table(rel_bias):
    j = np.arange(GRID_W)
    col_start = np.clip(j - NA_WIN_W // 2, 0, GRID_W - NA_WIN_W)
    col_valid = (j[None, :] >= col_start[:, None]) & (j[None, :] < col_start[:, None] + NA_WIN_W)
    col_idx = np.clip(j[None, :] - j[:, None] + NA_WIN_W - 1, 0, 2 * NA_WIN_W - 2)
    ds = np.arange(NA_WIN_H)[:, None] + np.arange(NA_WIN_H)[None, :]
    tab = rel_bias[:, ds][:, :, :, col_idx]
    tab = jnp.where(col_valid[None, None, None], tab, NEG)
    tab = tab.transpose(1, 0, 3, 2, 4)
    return tab.reshape(NA_WIN_H, D_HEADS, GRID_W, NA_WIN_H * GRID_W).astype(F32)


def _router_kernel(x_ref, sh_ref, sc_ref, wh_ref, wl_ref, b_ref, tri_ref, o_ref, cnt_ref, base_ref, *, spb):
    j = pl.program_id(0)

    @pl.when(j == 0)
    def _():
        base_ref[...] = jnp.zeros_like(base_ref)

    r = 1 + j // spb
    u = x_ref[...] * (1.0 + sc_ref[pl.ds(r, 1), :]) + sh_ref[pl.ds(r, 1), :]
    u_hi = u.astype(BF16)
    u_lo = (u - u_hi.astype(F32)).astype(BF16)
    wh = wh_ref[...]
    logits = _dot(u_hi, wh) + _dot(u_lo, wh) + _dot(u_hi, wl_ref[...]) + b_ref[...]
    lane = lax.broadcasted_iota(jnp.int32, (TM, LANES), 1)
    m1 = jnp.max(logits, axis=-1, keepdims=True)
    i1 = jnp.min(jnp.where(logits == m1, lane, LANES), axis=-1, keepdims=True)
    rest = jnp.where(lane == i1, NEG * 2, logits)
    m2 = jnp.max(rest, axis=-1, keepdims=True)
    i2 = jnp.min(jnp.where(rest == m2, lane, LANES), axis=-1, keepdims=True)
    e21 = jnp.exp(m2 - m1)
    g1 = 1.0 / (1.0 + e21)
    g2 = e21 / (1.0 + e21)
    oh1 = (lane == i1).astype(F32)
    oh2 = (lane == i2).astype(F32)
    both = oh1 + oh2
    before = _dot(tri_ref[...], both.astype(BF16)) + base_ref[0:1, :]
    r1 = jnp.sum(oh1 * before, axis=-1, keepdims=True)
    r2 = jnp.sum(oh2 * before, axis=-1, keepdims=True)
    out = jnp.zeros((TM, LANES), F32)
    for idx, col in enumerate((i1.astype(F32), i2.astype(F32), g1, g2, r1, r2)):
        out = jnp.where(lane == idx, col, out)
    o_ref[...] = out
    total = base_ref[0:1, :] + jnp.sum(both, axis=0, keepdims=True)
    base_ref[...] = jnp.broadcast_to(total, base_ref.shape)
    cnt_ref[...] = jnp.broadcast_to(total, cnt_ref.shape)


def _router(x1, modt, wr_hi, wr_lo, br, tri, *, nblk, spb):
    kern = functools.partial(_router_kernel, spb=spb)
    return pl.pallas_call(
        kern,
        grid=(nblk,),
        in_specs=[pl.BlockSpec((TM, D_MODEL), lambda j: (j, 0)),
                  _mod_spec(1, 3, 1), _mod_spec(1, 4, 1),
                  pl.BlockSpec((D_MODEL, LANES), lambda j: (0, 0)),
                  pl.BlockSpec((D_MODEL, LANES), lambda j: (0, 0)),
                  pl.BlockSpec((1, LANES), lambda j: (0, 0)),
                  pl.BlockSpec((TM, TM), lambda j: (0, 0))],
        out_specs=[pl.BlockSpec((TM, LANES), lambda j: (j, 0)),
                   pl.BlockSpec((8, LANES), lambda j: (0, 0))],
        out_shape=[jax.ShapeDtypeStruct((nblk * TM, LANES), F32),
                   jax.ShapeDtypeStruct((8, LANES), F32)],
        scratch_shapes=[pltpu.VMEM((8, LANES), F32)],
        compiler_params=_cparams(("arbitrary",), 24 << 20),
        name="moe_router",
    )(x1, modt, modt, wr_hi, wr_lo, br, tri)


def _dispatch_kernel(dest_ref, x_ref, sh_ref, sc_ref, xp_in_ref, xp_ref, u_ref, sem, *, spb):
    del xp_in_ref
    j = pl.program_id(0)
    r = 1 + j // spb
    u_ref[...] = x_ref[...] * (1.0 + sc_ref[pl.ds(r, 1), :]) + sh_ref[pl.ds(r, 1), :]

    def row_copy(i, k):
        d = dest_ref[0, k * TM + i]
        return pltpu.make_async_copy(u_ref.at[pl.ds(i, 1)], xp_ref.at[pl.ds(d, 1)], sem)

    def start(i, c):
        row_copy(i, 0).start(priority=0)
        row_copy(i, 1).start(priority=1)
        return c

    lax.fori_loop(0, TM, start, 0)

    def wait(i, c):
        row_copy(i, 0).wait()
        row_copy(i, 1).wait()
        return c

    lax.fori_loop(0, TM, wait, 0)


def _dispatch(dest_blk, x1, modt, xpad0, *, nblk, spb):
    kern = functools.partial(_dispatch_kernel, spb=spb)
    return pl.pallas_call(
        kern,
        grid=(nblk,),
        in_specs=[pl.BlockSpec((None, 1, 2 * TM), lambda j: (j, 0, 0), memory_space=pltpu.SMEM),
                  pl.BlockSpec((TM, D_MODEL), lambda j: (j, 0)),
                  _mod_spec(1, 3, 1), _mod_spec(1, 4, 1),
                  pl.BlockSpec(memory_space=pl.ANY)],
        out_specs=pl.BlockSpec(memory_space=pl.ANY),
        out_shape=jax.ShapeDtypeStruct(xpad0.shape, F32),
        scratch_shapes=[pltpu.VMEM((TM, D_MODEL), F32), pltpu.SemaphoreType.DMA(())],
        input_output_aliases={4: 0},
        compiler_params=_cparams(("arbitrary",), 24 << 20),
        name="moe_dispatch",
    )(dest_blk, x1, modt, modt, xpad0)


def _expert_kernel(be_ref, x_ref, wg_ref, wu_ref, wd_ref, o_ref, xb_ref):
    del be_ref
    f = pl.program_id(1)

    @pl.when(f == 0)
    def _():
        xb_ref[...] = x_ref[...].astype(BF16)

    xb = xb_ref[...]
    hg = _dot(xb, wg_ref[...])
    hu = _dot(xb, wu_ref[...])
    h = (hg * jax.nn.sigmoid(hg) * hu).astype(BF16)
    part = _dot(h, wd_ref[...])

    @pl.when(f == 0)
    def _():
        o_ref[...] = part

    @pl.when(f > 0)
    def _():
        o_ref[...] += part


def _experts(blk_expert, xpad, wg, wu, wd):
    npad = xpad.shape[0]
    nb = npad // MOE_BM
    nf = wg.shape[2] // MOE_TF
    grid_spec = pltpu.PrefetchScalarGridSpec(
        num_scalar_prefetch=1,
        grid=(nb, nf),
        in_specs=[pl.BlockSpec((MOE_BM, D_MODEL), lambda i, f, be: (i, 0)),
                  pl.BlockSpec((None, D_MODEL, MOE_TF), lambda i, f, be: (be[i], 0, f)),
                  pl.BlockSpec((None, D_MODEL, MOE_TF), lambda i, f, be: (be[i], 0, f)),
                  pl.BlockSpec((None, MOE_TF, D_MODEL), lambda i, f, be: (be[i], f, 0))],
        out_specs=pl.BlockSpec((MOE_BM, D_MODEL), lambda i, f, be: (i, 0)),
        scratch_shapes=[pltpu.VMEM((MOE_BM, D_MODEL), BF16)],
    )
    return pl.pallas_call(
        _expert_kernel,
        grid_spec=grid_spec,
        out_shape=jax.ShapeDtypeStruct((npad, D_MODEL), F32),
        compiler_params=_cparams(("arbitrary", "arbitrary"), 48 << 20),
        name="moe_experts",
    )(blk_expert, xpad, wg, wu, wd)


def _combine_kernel(dest_ref, yp_ref, x_ref, rt_ref, gate_ref, g_ref, beta_ref, o_ref, ybuf_ref, sem, *, spb):
    j = pl.program_id(0)
    r = 1 + j // spb

    def row_copy(i, k):
        d = dest_ref[0, k * TM + i]
        return pltpu.make_async_copy(yp_ref.at[pl.ds(d, 1)], ybuf_ref.at[k, pl.ds(i, 1)], sem)

    def start(i, c):
        row_copy(i, 0).start(priority=0)
        row_copy(i, 1).start(priority=1)
        return c

    lax.fori_loop(0, TM, start, 0)

    def wait(i, c):
        row_copy(i, 0).wait()
        row_copy(i, 1).wait()
        return c

    lax.fori_loop(0, TM, wait, 0)
    rt = rt_ref[...]
    g1 = rt[:, 2:3]
    g2 = rt[:, 3:4]
    y = g1 * ybuf_ref[0] + g2 * ybuf_ref[1]
    gate = gate_ref[pl.ds(r, 1), :]
    o_ref[...] = _layer_norm(ALPHA * x_ref[...] + gate * y, g_ref[...], beta_ref[...])


def _combine(dest_blk, ypad, x1, rt, modt, g, beta, *, nblk, spb):
    kern = functools.partial(_combine_kernel, spb=spb)
    return pl.pallas_call(
        kern,
        grid=(nblk,),
        in_specs=[pl.BlockSpec((None, 1, 2 * TM), lambda j: (j, 0, 0), memory_space=pltpu.SMEM),
                  pl.BlockSpec(memory_space=pl.ANY),
                  pl.BlockSpec((TM, D_MODEL), lambda j: (j, 0)),
                  pl.BlockSpec((TM, LANES), lambda j: (j, 0)),
                  _mod_spec(1, 5, 1),
                  pl.BlockSpec((1, D_MODEL), lambda j: (0, 0)),
                  pl.BlockSpec((1, D_MODEL), lambda j: (0, 0))],
        out_specs=pl.BlockSpec((TM, D_MODEL), lambda j: (j, 0)),
        out_shape=jax.ShapeDtypeStruct((nblk * TM, D_MODEL), F32),
        scratch_shapes=[pltpu.VMEM((2, TM, D_MODEL), F32), pltpu.SemaphoreType.DMA(())],
        compiler_params=_cparams(("arbitrary",), 24 << 20),
        name="moe_combine_ln",
    )(dest_blk, ypad, x1, rt, modt, g, beta)


def _head_perm(partner, heads, width):
    return np.concatenate([h * width + partner for h in range(heads)])


def _even_weights(w_in, q_gain, k_gain, partner):
    perm_q = _head_perm(partner, A_Q_HEADS, HEAD_DIM)
    perm_k = _head_perm(partner, A_KV_HEADS, HEAD_DIM)
    wq = w_in[:, :A_Q_W]
    wk = w_in[:, A_Q_W:A_Q_W + A_KV_W]
    w_ext = jnp.concatenate([w_in, wq[:, perm_q], wk[:, perm_k]], axis=1).astype(BF16)
    gvec = jnp.zeros((8, LANES), F32)
    gvec = gvec.at[0].set(jnp.tile(q_gain, 2)).at[1].set(jnp.tile(q_gain[partner], 2))
    gvec = gvec.at[2].set(jnp.tile(k_gain, 2)).at[3].set(jnp.tile(k_gain[partner], 2))
    return w_ext, gvec


def _block_diag(blocks):
    n = blocks.shape[0]
    w = blocks.shape[1]
    out = jnp.zeros((n * w, n * w), blocks.dtype)
    for g in range(n):
        out = out.at[g * w:(g + 1) * w, g * w:(g + 1) * w].set(blocks[g])
    return out


def _odd_weights(w_in, w_q_up, w_kv_up, partner):
    o_kr = C_Q_RANK + C_KV_RANK
    w_kr = w_in[:, o_kr:o_kr + C_ROPE]
    w_rest = jnp.concatenate([w_in[:, :o_kr], w_in[:, o_kr + C_ROPE:]], axis=1)
    kr_blk = jnp.concatenate([w_kr, w_kr[:, partner], jnp.zeros((D_MODEL, LANES - 2 * C_ROPE), F32)], axis=1)
    w_ext = jnp.concatenate([w_rest, kr_blk], axis=1).astype(BF16)
    dq = C_NOPE + C_ROPE
    wq3 = w_q_up.reshape(C_Q_RANK, C_HEADS, dq)
    zq = jnp.zeros((C_Q_RANK, C_HEADS, LANES - dq), F32)
    wq_pad = jnp.concatenate([wq3, zq], axis=-1).reshape(C_Q_RANK, C_HEADS * LANES)
    wq_rope_p = wq3[:, :, C_NOPE:][:, :, partner]
    wq_padp = jnp.concatenate([jnp.zeros((C_Q_RANK, C_HEADS, C_NOPE), F32), wq_rope_p, zq], axis=-1)
    wq_padp = wq_padp.reshape(C_Q_RANK, C_HEADS * LANES)
    wkv3 = w_kv_up.reshape(C_KV_RANK, C_HEADS, C_NOPE + C_V)
    wk_pad = jnp.concatenate([wkv3[:, :, :C_NOPE], jnp.zeros((C_KV_RANK, C_HEADS, LANES - C_NOPE), F32)], axis=-1)
    wk_pad = wk_pad.reshape(C_KV_RANK, C_HEADS * LANES)
    wv = wkv3[:, :, C_NOPE:].reshape(C_KV_RANK, C_HEADS * C_V)
    pl2 = np.zeros((LANES, C_HEADS * LANES), np.float32)
    for h in range(C_HEADS):
        for dd in range(C_ROPE):
            pl2[dd, h * LANES + C_NOPE + dd] = 1.0
            pl2[C_ROPE + dd, h * LANES + C_NOPE + dd] = 1.0
    return (w_ext, wq_pad.astype(BF16), wq_padp.astype(BF16), wk_pad.astype(BF16), wv.astype(BF16),
            jnp.asarray(pl2, BF16))


def kernel(x, c, ctx, c_ctx, w_ada, b_ada, ln1_g, ln1_b, ln2_g, ln2_b,
           ev_w_in, ev_w_out, ev_q_gain, ev_k_gain, ev_w_pool, ev_pool_scale,
           ev_w_gate, ev_w_up, ev_w_down,
           od_w_in, od_w_out, od_q_lat_gain, od_kv_lat_gain, od_w_q_up, od_w_kv_up, od_na_bias,
           od_w_router, od_b_router, od_w_gate, od_w_up, od_w_down):
    batch, seq, d = x.shape
    ctx_len = ctx.shape[1]
    assert d == D_MODEL and ctx_len == TM and seq % TM == 0 and seq % GRID_W == 0
    assert seq // GRID_W >= NA_WIN_H and batch + 1 <= 8
    nt = ctx_len + seq
    nbpb = nt // TM
    nblk = batch * nbpb
    spb = seq // TM
    nxb = batch * spb

    xa = jnp.concatenate([ctx, x], axis=1).reshape(batch * nt, d)
    cin = jnp.zeros((8, d), F32).at[0].set(c_ctx).at[1:1 + batch].set(c)
    modt = _ada_table(cin, w_ada, b_ada)

    cos64, sin64, partner64 = _rope_axis_tables(HEAD_DIM, seq, ctx_len)
    w_ext0, gvec = _even_weights(ev_w_in[0], ev_q_gain[0], ev_k_gain[0], partner64)
    cos2 = jnp.tile(cos64, (1, 2))
    sin2 = jnp.tile(sin64, (1, 2))
    ones_bd = _block_diag(jnp.ones((2, HEAD_DIM, HEAD_DIM), BF16))
    qt, kk, vt, pp = _even_inproj(xa, modt, w_ext0, cos2, sin2, gvec, ones_bd, batch=batch, nt=nt)
    attn = _flash(qt, kk.reshape(batch, nt, A_KV_W), vt, batch=batch, nt=nt, units=A_KV_HEADS // 2,
                  nq=2 * A_GROUP, kw=LANES, k_off=(0,) * (2 * A_GROUP),
                  v_off=tuple((jq // A_GROUP) * VROWS for jq in range(2 * A_GROUP)),
                  ctx_queries=True, name="gqa_flash")
    pooled = _pool(pp, _block_diag(ev_w_pool[0]).astype(BF16), ev_pool_scale[0].reshape(1, POOL_WIDTH),
                   batch=batch, nt=nt)
    w_out0 = ev_w_out[0].astype(BF16)
    x1 = _proj_ln(attn.reshape(batch * nt, A_Q_W), pooled, w_out0[:A_Q_W], w_out0[A_Q_W:], xa, modt,
                  ln1_g[0].reshape(1, d), ln1_b[0].reshape(1, d), layer=0, nblk=nblk,
                  x_map=lambda j: (j, 0), row_fn=lambda j: _mod_row_all(j, nbpb), name="even_out_ln")
    xa1 = _ffn_ln(x1, modt, ev_w_gate[0].astype(BF16), ev_w_up[0].astype(BF16), ev_w_down[0].astype(BF16),
                  ln2_g[0].reshape(1, d), ln2_b[0].reshape(1, d), nbpb=nbpb, nblk=nblk)

    cos32, sin32, partner32 = _rope_axis_tables(C_ROPE, seq, ctx_len)
    w_ext1, wq_pad, wq_padp, wk_pad, wv, pl2 = _odd_weights(od_w_in[0], od_w_q_up[0], od_w_kv_up[0], partner32)
    ntr = cos32.shape[0]
    pad_tail = jnp.zeros((ntr, LANES - C_NOPE - C_ROPE), F32)
    cos_q = jnp.concatenate([jnp.ones((ntr, C_NOPE), F32), cos32, pad_tail], axis=1)
    sin_q = jnp.concatenate([jnp.zeros((ntr, C_NOPE), F32), sin32, pad_tail], axis=1)
    cs_k = jnp.concatenate([cos32, sin32, jnp.zeros((ntr, LANES - 2 * C_ROPE), F32)], axis=1)
    qt1, k1, vt1, nq, nk, nv = _odd_inproj(
        xa1, modt, w_ext1, od_q_lat_gain[0].reshape(1, C_Q_RANK), od_kv_lat_gain[0].reshape(1, C_KV_RANK),
        wq_pad, wq_padp, wk_pad, wv, pl2, cos_q, sin_q, cs_k, batch=batch, nt=nt)
    mla = _flash(qt1, k1.reshape(batch, nt, C_HEADS * LANES), vt1, batch=batch, nt=nt, units=C_HEADS // 2,
                 nq=2, kw=2 * LANES, k_off=(0, LANES), v_off=(0, VROWS), ctx_queries=False, name="mla_flash")
    na = _na(nq, nk, nv, _na_bias_table(od_na_bias[0]), batch=batch, nt=nt, seq=seq)
    w_out1 = od_w_out[0].astype(BF16)
    x1b = _proj_ln(mla.reshape(batch * seq, C_HEADS * C_V), na.reshape(batch * seq, NA_W),
                   w_out1[:C_HEADS * C_V], w_out1[C_HEADS * C_V:], xa1, modt,
                   ln1_g[1].reshape(1, d), ln1_b[1].reshape(1, d), layer=1, nblk=nxb,
                   x_map=lambda j: (j + j // spb + 1, 0), row_fn=lambda j: 1 + j // spb, name="odd_out_ln")

    wr = jnp.zeros((d, LANES), F32).at[:, :N_EXPERTS].set(od_w_router[0])
    wr_hi = wr.astype(BF16)
    wr_lo = (wr - wr_hi.astype(F32)).astype(BF16)
    br = jnp.full((1, LANES), NEG, F32).at[0, :N_EXPERTS].set(od_b_router[0])
    tri = jnp.asarray(np.tril(np.ones((TM, TM), np.float32), -1), BF16)
    rt, cnt = _router(x1b, modt, wr_hi, wr_lo, br, tri, nblk=nxb, spb=spb)

    t_tok = batch * seq
    counts = cnt[0, :N_EXPERTS].astype(jnp.int32)
    padded = (counts + MOE_BM - 1) // MOE_BM * MOE_BM
    pad_end = jnp.cumsum(padded)
    pad_start = pad_end - padded
    n_blocks = -(-2 * t_tok // MOE_BM) + N_EXPERTS
    e12 = rt[:, 0:2].astype(jnp.int32)
    r12 = rt[:, 4:6].astype(jnp.int32)
    dest = pad_start[e12] + r12
    dest_blk = dest.reshape(nxb, TM, 2).transpose(0, 2, 1).reshape(nxb, 1, 2 * TM)
    blk_start = jnp.arange(n_blocks, dtype=jnp.int32) * MOE_BM
    blk_expert = jnp.minimum(jnp.sum((pad_end[None, :] <= blk_start[:, None]).astype(jnp.int32), axis=1),
                             N_EXPERTS - 1)

    xpad = _dispatch(dest_blk, x1b, modt, jnp.zeros((n_blocks * MOE_BM, d), F32), nblk=nxb, spb=spb)
    ypad = _experts(blk_expert, xpad, od_w_gate[0].astype(BF16), od_w_up[0].astype(BF16),
                    od_w_down[0].astype(BF16))
    out = _combine(dest_blk, ypad, x1b, rt, modt, ln2_g[1].reshape(1, d), ln2_b[1].reshape(1, d),
                   nblk=nxb, spb=spb)
    return out.reshape(batch, seq, d)
```

```python
import functools
import math

import numpy as np
import jax
import jax.numpy as jnp
from jax import lax
from jax.experimental import pallas as pl
from jax.experimental.pallas import tpu as pltpu

F32 = jnp.float32
BF16 = jnp.bfloat16

D_MODEL = 1024
DEPTH = 2
GRID_W = 64
ROPE_THETA = 10000.0
NORM_EPS = 1e-6
ALPHA = (2 * DEPTH) ** 0.25
HEAD_DIM = 64
A_Q_HEADS = 12
A_KV_HEADS = 4
A_GROUP = A_Q_HEADS // A_KV_HEADS
POOL_WINDOWS = (2, 4, 8, 16)
POOL_GROUP = 64
POOL_WIDTH = POOL_GROUP * len(POOL_WINDOWS)
C_HEADS = 8
C_Q_RANK = 384
C_KV_RANK = 256
C_NOPE = 64
C_ROPE = 32
C_V = 64
D_HEADS = 8
NA_WIN_H = 8
NA_WIN_W = 16
D_FF = 2816
N_EXPERTS = 8
D_FF_EXPERT = 3584
A_Q_W = A_Q_HEADS * HEAD_DIM
A_KV_W = A_KV_HEADS * HEAD_DIM
NA_W = D_HEADS * HEAD_DIM

LANES = 128
TM = 256
VMEM_CAP = 56 * 1024 * 1024
MOE_BM = 512
MOE_TF = 1792
MOE_NF = D_FF_EXPERT // MOE_TF
VROWS = HEAD_DIM + 16
FLASH_STAGES = 96
FLASH_LOOKAHEAD = 6
NEG = -1e30
LOG2E = 1.4426950408889634


def _cparams(sem, vmem_bytes):
    return pltpu.CompilerParams(dimension_semantics=sem,
                                vmem_limit_bytes=int(min(VMEM_CAP, max(vmem_bytes, 16 * 1024 * 1024))))


def _dot(a, b):
    return jnp.dot(a, b, preferred_element_type=F32)


def _dot_nt(a, b):
    return lax.dot_general(a, b, (((1,), (1,)), ((), ())), preferred_element_type=F32)


def _layer_norm(z, g, b):
    mu = jnp.mean(z, axis=-1, keepdims=True)
    zc = z - mu
    var = jnp.mean(zc * zc, axis=-1, keepdims=True)
    return zc * lax.rsqrt(var + NORM_EPS) * g + b


def _ada_kernel(c_ref, w_ref, b_ref, o_ref):
    c = c_ref[...]
    s = c * jax.nn.sigmoid(c)
    o_ref[...] = _dot(s.astype(BF16), w_ref[...].astype(BF16)) + b_ref[...]


def _ada_table(cin, w_ada, b_ada):
    depth, d, n = w_ada.shape
    return pl.pallas_call(
        _ada_kernel,
        grid=(depth, n // d),
        in_specs=[pl.BlockSpec((8, d), lambda l, j: (0, 0)),
                  pl.BlockSpec((None, d, d), lambda l, j: (l, 0, j)),
                  pl.BlockSpec((None, 1, d), lambda l, j: (l, 0, j))],
        out_specs=pl.BlockSpec((None, 8, d), lambda l, j: (l, 0, j)),
        out_shape=jax.ShapeDtypeStruct((depth, 8, n), F32),
        compiler_params=_cparams(("arbitrary", "arbitrary"), 24 << 20),
        name="ada_table",
    )(cin, w_ada, b_ada.reshape(depth, 1, n))


def _mod_spec(layer, k, nargs):
    if nargs == 1:
        return pl.BlockSpec((None, 8, D_MODEL), lambda j: (layer, 0, k))
    return pl.BlockSpec((None, 8, D_MODEL), lambda *_: (layer, 0, k))


def _mod_row_all(j, nbpb):
    return jnp.where(j % nbpb == 0, 0, 1 + j // nbpb)


def _rope_axis_tables(rope_dims, seq, ctx):
    n = rope_dims // 2
    half = n // 2
    t = jnp.arange(seq, dtype=jnp.int32)
    row = (t // GRID_W).astype(F32)
    col = (t % GRID_W).astype(F32)
    inv = ROPE_THETA ** (-jnp.arange(half, dtype=F32) / half)
    cols_c, cols_s = [], []
    for pos in (row, col):
        ang = pos[:, None] * inv[None, :]
        c, s = jnp.cos(ang), jnp.sin(ang)
        cols_c += [c, c]
        cols_s += [-s, s]
    cos = jnp.concatenate(cols_c, axis=-1)
    sin = jnp.concatenate(cols_s, axis=-1)
    cos = jnp.concatenate([jnp.ones((ctx, rope_dims), F32), cos], axis=0)
    sin = jnp.concatenate([jnp.zeros((ctx, rope_dims), F32), sin], axis=0)
    d = np.arange(rope_dims)
    partner = (d // n) * n + ((d % n) + half) % n
    return cos, sin, partner


def _store_vt(vt_ref, vt, heads):
    ones = jnp.ones((VROWS - HEAD_DIM, TM), BF16)
    for h in range(heads):
        vt_ref[h * VROWS:h * VROWS + HEAD_DIM, :] = vt[h * HEAD_DIM:(h + 1) * HEAD_DIM, :]
        vt_ref[h * VROWS + HEAD_DIM:(h + 1) * VROWS, :] = ones


def _even_inproj_kernel(x_ref, sh_ref, sc_ref, w_ref, cos_ref, sin_ref, gv_ref, ones_ref,
                        qt_ref, k_ref, vt_ref, p_ref, *, nbpb):
    j = pl.program_id(0)
    r = _mod_row_all(j, nbpb)
    shift = sh_ref[pl.ds(r, 1), :]
    scale = sc_ref[pl.ds(r, 1), :]
    u = x_ref[...] * (1.0 + scale) + shift
    acc = _dot(u.astype(BF16), w_ref[...])
    cos = cos_ref[...]
    sin = sin_ref[...]
    ones_bd = ones_ref[...]

    def norm_rope(a, ap, g, gp):
        ssq = _dot((a * a).astype(BF16), ones_bd)
        rinv = lax.rsqrt(ssq * (1.0 / HEAD_DIM) + NORM_EPS)
        return (a * g * cos + ap * gp * sin) * rinv

    gq, gqp, gk, gkp = gv_ref[0:1, :], gv_ref[1:2, :], gv_ref[2:3, :], gv_ref[3:4, :]
    p_off = A_Q_W + 2 * A_KV_W + POOL_WIDTH
    zeros_half = jnp.zeros((HEAD_DIM, TM), BF16)
    for c in range(A_Q_W // LANES):
        a = acc[:, c * LANES:(c + 1) * LANES]
        ap = acc[:, p_off + c * LANES:p_off + (c + 1) * LANES]
        rot = norm_rope(a, ap, gq, gqp) * (HEAD_DIM ** -0.5 * LOG2E)
        rot_t = rot.T
        for hl in range(2):
            head = 2 * c + hl
            slot = (head // A_GROUP) % 2
            base = head * LANES
            qt_ref[base + slot * HEAD_DIM:base + (slot + 1) * HEAD_DIM, :] = (
                rot_t[hl * HEAD_DIM:(hl + 1) * HEAD_DIM, :].astype(BF16))
            qt_ref[base + (1 - slot) * HEAD_DIM:base + (2 - slot) * HEAD_DIM, :] = zeros_half
    for c in range(A_KV_W // LANES):
        a = acc[:, A_Q_W + c * LANES:A_Q_W + (c + 1) * LANES]
        ap = acc[:, p_off + A_Q_W + c * LANES:p_off + A_Q_W + (c + 1) * LANES]
        k_ref[:, c * LANES:(c + 1) * LANES] = norm_rope(a, ap, gk, gkp).astype(BF16)
    v = acc[:, A_Q_W + A_KV_W:A_Q_W + 2 * A_KV_W]
    _store_vt(vt_ref, v.T.astype(BF16), A_KV_HEADS)
    p_ref[...] = acc[:, A_Q_W + 2 * A_KV_W:A_Q_W + 2 * A_KV_W + POOL_WIDTH]


def _even_inproj(xa, modt, w_ext, cos2, sin2, gvec, ones_bd, *, batch, nt):
    nbpb = nt // TM
    nblk = batch * nbpb
    n_ext = w_ext.shape[1]
    kern = functools.partial(_even_inproj_kernel, nbpb=nbpb)
    return pl.pallas_call(
        kern,
        grid=(nblk,),
        in_specs=[pl.BlockSpec((TM, D_MODEL), lambda j: (j, 0)),
                  _mod_spec(0, 0, 1), _mod_spec(0, 1, 1),
                  pl.BlockSpec((D_MODEL, n_ext), lambda j: (0, 0)),
                  pl.BlockSpec((TM, LANES), lambda j: (j % nbpb, 0)),
                  pl.BlockSpec((TM, LANES), lambda j: (j % nbpb, 0)),
                  pl.BlockSpec((8, LANES), lambda j: (0, 0)),
                  pl.BlockSpec((LANES, LANES), lambda j: (0, 0))],
        out_specs=[pl.BlockSpec((None, A_Q_HEADS * LANES, TM), lambda j: (j // nbpb, 0, j % nbpb)),
                   pl.BlockSpec((TM, A_KV_W), lambda j: (j, 0)),
                   pl.BlockSpec((None, None, A_KV_HEADS * VROWS, TM), lambda j: (j // nbpb, j % nbpb, 0, 0)),
                   pl.BlockSpec((TM, POOL_WIDTH), lambda j: (j, 0))],
        out_shape=[jax.ShapeDtypeStruct((batch, A_Q_HEADS * LANES, nt), BF16),
                   jax.ShapeDtypeStruct((batch * nt, A_KV_W), BF16),
                   jax.ShapeDtypeStruct((batch, nbpb, A_KV_HEADS * VROWS, TM), BF16),
                   jax.ShapeDtypeStruct((batch * nt, POOL_WIDTH), F32)],
        compiler_params=_cparams(("arbitrary",), 40 << 20),
        name="even_inproj",
    )(xa, modt, modt, w_ext, cos2, sin2, gvec, ones_bd)


def _flash_kernel(q_ref, k_ref, v_ref, o_ref, acc_ref, m_ref, out_ref, *, nq, k_off, v_off, ctx_queries, nsub):
    nch = v_ref.shape[0]
    acc_ref[...] = jnp.zeros_like(acc_ref)
    m_ref[...] = jnp.full_like(m_ref, NEG)

    def scores(j, c):
        start = pl.multiple_of(c * TM, TM)
        kc = k_ref[pl.ds(start, TM), k_off[j]:k_off[j] + LANES]
        return _dot(kc, q_ref[j * LANES:(j + 1) * LANES, :])

    def update(j, c, s):
        row = slice(j, j + 1)
        m_old = m_ref[row, :]
        m_new = jnp.maximum(m_old, jnp.max(s, axis=0, keepdims=True))
        alpha = jnp.exp2(m_old - m_new)
        p = jnp.exp2(s - m_new)
        m_ref[row, :] = m_new
        pv = _dot(v_ref[c, v_off[j]:v_off[j] + VROWS, :], p.astype(BF16))
        rows = slice(j * VROWS, (j + 1) * VROWS)
        acc_ref[rows, :] = alpha * acc_ref[rows, :] + pv

    def run(chunks):
        stages = [(j, c) for c in chunks for j in range(nq)]
        ahead = [scores(*st) for st in stages[:FLASH_LOOKAHEAD]]
        for i, (j, c) in enumerate(stages):
            if i + FLASH_LOOKAHEAD < len(stages):
                ahead.append(scores(*stages[i + FLASH_LOOKAHEAD]))
            update(j, c, ahead.pop(0))

    run([0])
    n_main = (nch - 1) // nsub
    if ctx_queries:
        n_main = jnp.where(pl.program_id(2) == 0, 0, n_main)

    def body(it, carry):
        run([1 + it * nsub + t for t in range(nsub)])
        return carry

    lax.fori_loop(0, n_main, body, 0)
    for j in range(nq):
        denom = acc_ref[j * VROWS + HEAD_DIM:j * VROWS + HEAD_DIM + 1, :]
        out_ref[j * HEAD_DIM:(j + 1) * HEAD_DIM, :] = acc_ref[j * VROWS:j * VROWS + HEAD_DIM, :] / denom
    o_ref[...] = out_ref[...].T.astype(BF16)


def _flash(qt, k, vt, *, batch, nt, units, nq, kw, k_off, v_off, ctx_queries, name):
    nbpb = nt // TM
    nqb = nbpb if ctx_queries else nbpb - 1
    qoff = 0 if ctx_queries else 1
    nv = vt.shape[2] // units
    nsub = max(d for d in range(1, FLASH_STAGES // nq + 1) if (nbpb - 1) % d == 0)
    kern = functools.partial(_flash_kernel, nq=nq, k_off=k_off, v_off=v_off, ctx_queries=ctx_queries, nsub=nsub)
    vmem = 2 * (nt * kw * 2 + nt * nv * 2) + 4 * nq * LANES * TM * 2 + 3 * nq * HEAD_DIM * TM * 4 + (8 << 20)
    return pl.pallas_call(
        kern,
        grid=(batch, units, nqb),
        in_specs=[pl.BlockSpec((None, nq * LANES, TM), lambda b, u, i: (b, u, i + qoff)),
                  pl.BlockSpec((None, nt, kw), lambda b, u, i: (b, 0, u)),
                  pl.BlockSpec((None, nbpb, nv, TM), lambda b, u, i: (b, 0, u, 0))],
        out_specs=pl.BlockSpec((None, TM, nq * HEAD_DIM), lambda b, u, i: (b, i, u)),
        out_shape=jax.ShapeDtypeStruct((batch, nqb * TM, units * nq * HEAD_DIM), BF16),
        scratch_shapes=[pltpu.VMEM((nq * VROWS, TM), F32),
                        pltpu.VMEM((8, TM), F32),
                        pltpu.VMEM((nq * HEAD_DIM, TM), F32)],
        compiler_params=_cparams(("arbitrary", "arbitrary", "arbitrary"), vmem),
        name=name,
    )(qt, k, vt)


def _pool_kernel(prev_ref, cur_ref, next_ref, w_ref, ps_ref, o_ref, *, nbpb, nt):
    jj = pl.program_id(0) % nbpb
    base = jj * TM
    seg_lo = jnp.where(jj == 0, 0, TM)
    seg_hi = jnp.where(jj == 0, TM, nt)
    cur = cur_ref[...]
    xcat = jnp.concatenate([prev_ref[...], cur, next_ref[...]], axis=0)
    x_hi = xcat.astype(BF16)
    x_lo = (xcat - x_hi.astype(F32)).astype(BF16)
    t = base + lax.broadcasted_iota(jnp.int32, (TM, 3 * TM), 0)
    s = base - TM + lax.broadcasted_iota(jnp.int32, (TM, 3 * TM), 1)
    t1 = base + lax.broadcasted_iota(jnp.int32, (TM, 1), 0)
    lane_grp = lax.broadcasted_iota(jnp.int32, (TM, POOL_WIDTH), 1) // POOL_GROUP
    mean = jnp.zeros((TM, POOL_WIDTH), F32)
    for g, win in enumerate(POOL_WINDOWS):
        lo = win // 2
        hi = win - lo
        valid = ((s >= jnp.maximum(t - lo, seg_lo)) & (s < jnp.minimum(t + hi, seg_hi)))
        vb = valid.astype(F32).astype(BF16)
        wsum = _dot(vb, x_hi) + _dot(vb, x_lo)
        cnt = (jnp.minimum(t1 + hi, seg_hi) - jnp.maximum(t1 - lo, seg_lo)).astype(F32)
        mean = jnp.where(lane_grp == g, wsum / cnt, mean)
    y = (mean - cur).astype(BF16)
    o_ref[...] = (_dot(y, w_ref[...]) * ps_ref[...]).astype(BF16)


def _pool(p, w_bd, ps, *, batch, nt):
    nbpb = nt // TM
    nblk = batch * nbpb
    kern = functools.partial(_pool_kernel, nbpb=nbpb, nt=nt)

    def prev_map(j):
        return (jnp.maximum(j - 1, 0), 0)

    def next_map(j):
        return (jnp.minimum(j + 1, nblk - 1), 0)

    return pl.pallas_call(
        kern,
        grid=(nblk,),
        in_specs=[pl.BlockSpec((TM, POOL_WIDTH), prev_map),
                  pl.BlockSpec((TM, POOL_WIDTH), lambda j: (j, 0)),
                  pl.BlockSpec((TM, POOL_WIDTH), next_map),
                  pl.BlockSpec((POOL_WIDTH, POOL_WIDTH), lambda j: (0, 0)),
                  pl.BlockSpec((1, POOL_WIDTH), lambda j: (0, 0))],
        out_specs=pl.BlockSpec((TM, POOL_WIDTH), lambda j: (j, 0)),
        out_shape=jax.ShapeDtypeStruct((batch * nt, POOL_WIDTH), BF16),
        compiler_params=_cparams(("arbitrary",), 24 << 20),
        name="pool_mixer",
    )(p, p, p, w_bd, ps)


def _proj_ln_kernel(a_ref, b_ref, wa_ref, wb_ref, x_ref, gate_ref, g_ref, beta_ref, o_ref, *, row_fn):
    r = row_fn(pl.program_id(0))
    gate = gate_ref[pl.ds(r, 1), :]
    y = _dot(a_ref[...], wa_ref[...]) + _dot(b_ref[...], wb_ref[...])
    o_ref[...] = _layer_norm(ALPHA * x_ref[...] + gate * y, g_ref[...], beta_ref[...])


def _proj_ln(a, b, wa, wb, x, modt, g, beta, *, layer, nblk, x_map, row_fn, name):
    ka, kb = a.shape[1], b.shape[1]
    kern = functools.partial(_proj_ln_kernel, row_fn=row_fn)
    return pl.pallas_call(
        kern,
        grid=(nblk,),
        in_specs=[pl.BlockSpec((TM, ka), lambda j: (j, 0)),
                  pl.BlockSpec((TM, kb), lambda j: (j, 0)),
                  pl.BlockSpec((ka, D_MODEL), lambda j: (0, 0)),
                  pl.BlockSpec((kb, D_MODEL), lambda j: (0, 0)),
                  pl.BlockSpec((TM, D_MODEL), x_map),
                  _mod_spec(layer, 2, 1),
                  pl.BlockSpec((1, D_MODEL), lambda j: (0, 0)),
                  pl.BlockSpec((1, D_MODEL), lambda j: (0, 0))],
        out_specs=pl.BlockSpec((TM, D_MODEL), lambda j: (j, 0)),
        out_shape=jax.ShapeDtypeStruct((nblk * TM, D_MODEL), F32),
        compiler_params=_cparams(("arbitrary",), 32 << 20),
        name=name,
    )(a, b, wa, wb, x, modt, g, beta)


def _ffn_ln_kernel(x_ref, sh_ref, sc_ref, gate_ref, wg_ref, wu_ref, wd_ref, g_ref, beta_ref, o_ref, *, nbpb):
    r = _mod_row_all(pl.program_id(0), nbpb)
    x = x_ref[...]
    u = (x * (1.0 + sc_ref[pl.ds(r, 1), :]) + sh_ref[pl.ds(r, 1), :]).astype(BF16)
    hg = _dot(u, wg_ref[...])
    hu = _dot(u, wu_ref[...])
    h = (hg * jax.nn.sigmoid(hg) * hu).astype(BF16)
    y = _dot(h, wd_ref[...])
    gate = gate_ref[pl.ds(r, 1), :]
    o_ref[...] = _layer_norm(ALPHA * x + gate * y, g_ref[...], beta_ref[...])


def _resident(shape):
    return pl.BlockSpec(shape, lambda *_: (0,) * len(shape), pipeline_mode=pl.Buffered(1))


def _ffn_ln(x1, modt, wg, wu, wd, g, beta, *, nbpb, nblk):
    dff = wg.shape[1]
    kern = functools.partial(_ffn_ln_kernel, nbpb=nbpb)
    return pl.pallas_call(
        kern,
        grid=(nblk,),
        in_specs=[pl.BlockSpec((TM, D_MODEL), lambda j: (j, 0)),
                  _mod_spec(0, 3, 1), _mod_spec(0, 4, 1), _mod_spec(0, 5, 1),
                  _resident((D_MODEL, dff)), _resident((D_MODEL, dff)), _resident((dff, D_MODEL)),
                  pl.BlockSpec((1, D_MODEL), lambda j: (0, 0)),
                  pl.BlockSpec((1, D_MODEL), lambda j: (0, 0))],
        out_specs=pl.BlockSpec((TM, D_MODEL), lambda j: (j, 0)),
        out_shape=jax.ShapeDtypeStruct((nblk * TM, D_MODEL), F32),
        compiler_params=_cparams(("arbitrary",), 44 << 20),
        name="even_ffn_ln",
    )(x1, modt, modt, modt, wg, wu, wd, g, beta)


def _odd_inproj_kernel(x_ref, sh_ref, sc_ref, w_ref, gq_ref, gkv_ref, wq_ref, wqp_ref, wk_ref, wv_ref,
                       pl2_ref, cq_ref, sq_ref, cs_ref,
                       qt_ref, k_ref, vt_ref, nq_ref, nk_ref, nv_ref, *, nbpb):
    j = pl.program_id(0)
    r = _mod_row_all(j, nbpb)
    u = x_ref[...] * (1.0 + sc_ref[pl.ds(r, 1), :]) + sh_ref[pl.ds(r, 1), :]
    acc = _dot(u.astype(BF16), w_ref[...])
    o = 0
    cq = acc[:, o:o + C_Q_RANK]; o += C_Q_RANK
    ckv = acc[:, o:o + C_KV_RANK]; o += C_KV_RANK
    nq = acc[:, o:o + NA_W]; o += NA_W
    nk = acc[:, o:o + NA_W]; o += NA_W
    nv = acc[:, o:o + NA_W]; o += NA_W
    krc = acc[:, o:o + LANES]

    def rms(a, g):
        return (a * lax.rsqrt(jnp.mean(a * a, axis=-1, keepdims=True) + NORM_EPS) * g).astype(BF16)

    cqn = rms(cq, gq_ref[...])
    kvn = rms(ckv, gkv_ref[...])
    cos_q = jnp.tile(cq_ref[...], (1, C_HEADS))
    sin_q = jnp.tile(sq_ref[...], (1, C_HEADS))
    q = _dot(cqn, wq_ref[...]) * cos_q + _dot(cqn, wqp_ref[...]) * sin_q
    q = q * ((C_NOPE + C_ROPE) ** -0.5 * LOG2E)
    qt_ref[...] = q.T.astype(BF16)
    kr_terms = (krc * cs_ref[...]).astype(BF16)
    k_ref[...] = (_dot(kvn, wk_ref[...]) + _dot(kr_terms, pl2_ref[...])).astype(BF16)
    _store_vt(vt_ref, _dot(kvn, wv_ref[...]).T.astype(BF16), C_HEADS)
    nq_ref[...] = (nq * HEAD_DIM ** -0.5).astype(BF16)
    nk_ref[...] = nk.astype(BF16)
    nv_ref[...] = nv.astype(BF16)


def _odd_inproj(xa, modt, w_ext, gq, gkv, wq_pad, wq_padp, wk_pad, wv, pl2, cos_q, sin_q, cs_k, *, batch, nt):
    nbpb = nt // TM
    nblk = batch * nbpb
    n_ext = w_ext.shape[1]
    hq = C_HEADS * LANES
    hv = C_HEADS * C_V
    kern = functools.partial(_odd_inproj_kernel, nbpb=nbpb)
    const = lambda j: (0, 0)
    tok = lambda j: (j, 0)
    pos = lambda j: (j % nbpb, 0)
    return pl.pallas_call(
        kern,
        grid=(nblk,),
        in_specs=[pl.BlockSpec((TM, D_MODEL), tok),
                  _mod_spec(1, 0, 1), _mod_spec(1, 1, 1),
                  pl.BlockSpec((D_MODEL, n_ext), const),
                  pl.BlockSpec((1, C_Q_RANK), const),
                  pl.BlockSpec((1, C_KV_RANK), const),
                  pl.BlockSpec((C_Q_RANK, hq), const),
                  pl.BlockSpec((C_Q_RANK, hq), const),
                  pl.BlockSpec((C_KV_RANK, hq), const),
                  pl.BlockSpec((C_KV_RANK, hv), const),
                  pl.BlockSpec((LANES, hq), const),
                  pl.BlockSpec((TM, LANES), pos),
                  pl.BlockSpec((TM, LANES), pos),
                  pl.BlockSpec((TM, LANES), pos)],
        out_specs=[pl.BlockSpec((None, hq, TM), lambda j: (j // nbpb, 0, j % nbpb)),
                   pl.BlockSpec((TM, hq), tok),
                   pl.BlockSpec((None, None, C_HEADS * VROWS, TM), lambda j: (j // nbpb, j % nbpb, 0, 0)),
                   pl.BlockSpec((TM, NA_W), tok),
                   pl.BlockSpec((TM, NA_W), tok),
                   pl.BlockSpec((TM, NA_W), tok)],
        out_shape=[jax.ShapeDtypeStruct((batch, hq, nt), BF16),
                   jax.ShapeDtypeStruct((batch * nt, hq), BF16),
                   jax.ShapeDtypeStruct((batch, nbpb, C_HEADS * VROWS, TM), BF16),
                   jax.ShapeDtypeStruct((batch * nt, NA_W), BF16),
                   jax.ShapeDtypeStruct((batch * nt, NA_W), BF16),
                   jax.ShapeDtypeStruct((batch * nt, NA_W), BF16)],
        compiler_params=_cparams(("arbitrary",), 48 << 20),
        name="odd_inproj",
    )(xa, modt, modt, w_ext, gq, gkv, wq_pad, wq_padp, wk_pad, wv, pl2, cos_q, sin_q, cs_k)


def _na_kernel(q_ref, kb_ref, vb_ref, kc_ref, vc_ref, bias_ref, o_ref):
    lane = lax.broadcasted_iota(jnp.int32, (GRID_W, LANES), 1)
    first_half = (lane // HEAD_DIM) == 0
    scores = []
    for h in range(D_HEADS):
        cols = slice((h // 2) * LANES, (h // 2 + 1) * LANES)
        qp = q_ref[:, cols]
        qh = jnp.where(first_half == (h % 2 == 0), qp, jnp.zeros_like(qp))
        s_w = _dot_nt(qh, kb_ref[0, :, cols]) + bias_ref[h]
        s_c = _dot_nt(qh, kc_ref[:, cols])
        scores.append((s_w, s_c))
    outs = []
    for h in range(D_HEADS):
        cols = slice((h // 2) * LANES, (h // 2 + 1) * LANES)
        s_w, s_c = scores[h]
        m = jnp.maximum(jnp.max(s_w, axis=-1, keepdims=True), jnp.max(s_c, axis=-1, keepdims=True))
        p_w = jnp.exp(s_w - m)
        p_c = jnp.exp(s_c - m)
        l = jnp.sum(p_w, axis=-1, keepdims=True) + jnp.sum(p_c, axis=-1, keepdims=True)
        o = _dot(p_w.astype(BF16), vb_ref[0, :, cols]) + _dot(p_c.astype(BF16), vc_ref[:, cols])
        outs.append(o / l)
    for hp in range(D_HEADS // 2):
        o_ref[:, hp * LANES:(hp + 1) * LANES] = jnp.where(first_half, outs[2 * hp], outs[2 * hp + 1]).astype(BF16)


def _na(nq, nk, nv, bias_tab, *, batch, nt, seq):
    rows = seq // GRID_W
    kh = NA_WIN_H
    band = kh * GRID_W
    ctx_rows = TM // GRID_W

    def rs_of(r):
        return jnp.clip(r - kh // 2, 0, rows - kh)

    nq3 = nq.reshape(batch, nt, NA_W)
    nk3 = nk.reshape(batch, nt, NA_W)
    nv3 = nv.reshape(batch, nt, NA_W)
    band_spec = pl.BlockSpec((pl.Element(1), pl.Element(band), pl.Element(NA_W)),
                             lambda b, r: (b, pl.multiple_of(TM + rs_of(r) * GRID_W, GRID_W), 0))
    ctx_spec = pl.BlockSpec((None, TM, NA_W), lambda b, r: (b, 0, 0))
    return pl.pallas_call(
        _na_kernel,
        grid=(batch, rows),
        in_specs=[pl.BlockSpec((None, GRID_W, NA_W), lambda b, r: (b, ctx_rows + r, 0)),
                  band_spec, band_spec, ctx_spec, ctx_spec,
                  pl.BlockSpec((None, D_HEADS, GRID_W, band), lambda b, r: (rs_of(r) - r + NA_WIN_H - 1, 0, 0, 0))],
        out_specs=pl.BlockSpec((None, GRID_W, NA_W), lambda b, r: (b, r, 0)),
        out_shape=jax.ShapeDtypeStruct((batch, seq, NA_W), BF16),
        compiler_params=_cparams(("arbitrary", "arbitrary"), 24 << 20),
        name="neighborhood_attn",
    )(nq3, nk3, nv3, nk3, nv3, bias_tab)


def _na_bias_table(rel_bias):
    j = np.arange(GRID_W)
    col_start = np.clip(j - NA_WIN_W // 2, 0, GRID_W - NA_WIN_W)
    col_valid = (j[None, :] >= col_start[:, None]) & (j[None, :] < col_start[:, None] + NA_WIN_W)
    col_idx = np.clip(j[None, :] - j[:, None] + NA_WIN_W - 1, 0, 2 * NA_WIN_W - 2)
    ds = np.arange(NA_WIN_H)[:, None] + np.arange(NA_WIN_H)[None, :]
    onehot = jnp.asarray(col_idx[:, :, None] == np.arange(2 * NA_WIN_W - 1), F32)
    tab = jnp.einsum('hdic,qkc->hdiqk', rel_bias[:, ds], onehot,
                     precision=lax.Precision.HIGHEST)
    tab = jnp.where(col_valid[None, None, None], tab, NEG)
    tab = tab.transpose(1, 0, 3, 2, 4)
    return tab.reshape(NA_WIN_H, D_HEADS, GRID_W, NA_WIN_H * GRID_W).astype(F32)


def _router_kernel(x_ref, sh_ref, sc_ref, wh_ref, wl_ref, b_ref, tri_ref, o_ref, cnt_ref, u_ref, base_ref, *, spb):
    j = pl.program_id(0)

    @pl.when(j == 0)
    def _():
        base_ref[...] = jnp.zeros_like(base_ref)

    r = 1 + j // spb
    u = x_ref[...] * (1.0 + sc_ref[pl.ds(r, 1), :]) + sh_ref[pl.ds(r, 1), :]
    u_ref[...] = u
    u_hi = u.astype(BF16)
    u_lo = (u - u_hi.astype(F32)).astype(BF16)
    wh = wh_ref[...]
    logits = _dot(u_hi, wh) + _dot(u_lo, wh) + _dot(u_hi, wl_ref[...]) + b_ref[...]
    lane = lax.broadcasted_iota(jnp.int32, (TM, LANES), 1)
    m1 = jnp.max(logits, axis=-1, keepdims=True)
    i1 = jnp.min(jnp.where(logits == m1, lane, LANES), axis=-1, keepdims=True)
    rest = jnp.where(lane == i1, NEG * 2, logits)
    m2 = jnp.max(rest, axis=-1, keepdims=True)
    i2 = jnp.min(jnp.where(rest == m2, lane, LANES), axis=-1, keepdims=True)
    e21 = jnp.exp(m2 - m1)
    g1 = 1.0 / (1.0 + e21)
    g2 = e21 / (1.0 + e21)
    oh1 = (lane == i1).astype(F32)
    oh2 = (lane == i2).astype(F32)
    both = oh1 + oh2
    before = _dot(tri_ref[...], both.astype(BF16)) + base_ref[0:1, :]
    r1 = jnp.sum(oh1 * before, axis=-1, keepdims=True)
    r2 = jnp.sum(oh2 * before, axis=-1, keepdims=True)
    out = jnp.zeros((TM, LANES), F32)
    for idx, col in enumerate((i1.astype(F32), i2.astype(F32), g1, g2, r1, r2)):
        out = jnp.where(lane == idx, col, out)
    o_ref[...] = out
    total = base_ref[0:1, :] + jnp.sum(both, axis=0, keepdims=True)
    base_ref[...] = jnp.broadcast_to(total, base_ref.shape)
    cnt_ref[...] = jnp.broadcast_to(total, cnt_ref.shape)


def _router(x1, modt, wr_hi, wr_lo, br, tri, *, nblk, spb):
    kern = functools.partial(_router_kernel, spb=spb)
    return pl.pallas_call(
        kern,
        grid=(nblk,),
        in_specs=[pl.BlockSpec((TM, D_MODEL), lambda j: (j, 0)),
                  _mod_spec(1, 3, 1), _mod_spec(1, 4, 1),
                  pl.BlockSpec((D_MODEL, LANES), lambda j: (0, 0)),
                  pl.BlockSpec((D_MODEL, LANES), lambda j: (0, 0)),
                  pl.BlockSpec((1, LANES), lambda j: (0, 0)),
                  pl.BlockSpec((TM, TM), lambda j: (0, 0))],
        out_specs=[pl.BlockSpec((TM, LANES), lambda j: (j, 0)),
                   pl.BlockSpec((8, LANES), lambda j: (0, 0)),
                   pl.BlockSpec((TM, D_MODEL), lambda j: (j, 0))],
        out_shape=[jax.ShapeDtypeStruct((nblk * TM, LANES), F32),
                   jax.ShapeDtypeStruct((8, LANES), F32),
                   jax.ShapeDtypeStruct((nblk * TM, D_MODEL), F32)],
        scratch_shapes=[pltpu.VMEM((8, LANES), F32)],
        compiler_params=_cparams(("arbitrary",), 24 << 20),
        name="moe_router",
    )(x1, modt, modt, wr_hi, wr_lo, br, tri)


def _expert_kernel(be_ref, src0_ref, srcn_ref, dstp_ref, u_ref, wg_ref, wu_ref, wd_ref, y_ref,
                   xbuf_ref, xb_ref, obuf_ref, gsem, ssem):
    del be_ref
    i = pl.program_id(0)
    f = pl.program_id(1)
    last_i = pl.num_programs(0) - 1
    slot = i % 2
    other = 1 - slot
    half = MOE_BM // MOE_NF

    def gather_row(src_ref, r, dst_slot):
        return pltpu.make_async_copy(u_ref.at[pl.ds(src_ref[0, r], 1)],
                                     xbuf_ref.at[dst_slot, pl.ds(r, 1)], gsem.at[dst_slot])

    def scatter_row(r, src_slot):
        return pltpu.make_async_copy(obuf_ref.at[src_slot, pl.ds(r, 1)],
                                     y_ref.at[pl.ds(dstp_ref[0, r], 1)], ssem.at[src_slot])

    def gather_block_wait(s):
        pltpu.make_async_copy(u_ref.at[pl.ds(0, MOE_BM)], xbuf_ref.at[s], gsem.at[s]).wait()

    def scatter_block_wait(s):
        pltpu.make_async_copy(obuf_ref.at[s], y_ref.at[pl.ds(0, MOE_BM)], ssem.at[s]).wait()

    @pl.when((i == 0) & (f == 0))
    def _():
        obuf_ref[...] = jnp.zeros_like(obuf_ref)

        def first(r, c):
            gather_row(src0_ref, r, 0).start()
            return c

        lax.fori_loop(0, MOE_BM, first, 0)

    @pl.when(f == 0)
    def _():
        gather_block_wait(slot)
        xb_ref[...] = xbuf_ref[slot].astype(BF16)

    @pl.when((f == 0) & (i >= 1))
    def _():
        scatter_block_wait(slot)

    for t in range(half):
        r = f * half + t
        gather_row(srcn_ref, r, other).start(priority=0)
        scatter_row(r, other).start(priority=1)

    xb = xb_ref[...]
    hg = _dot(xb, wg_ref[...])
    hu = _dot(xb, wu_ref[...])
    h = (hg * jax.nn.sigmoid(hg) * hu).astype(BF16)
    part = _dot(h, wd_ref[...])

    @pl.when(f == 0)
    def _():
        obuf_ref[slot] = part

    @pl.when(f > 0)
    def _():
        obuf_ref[slot] += part

    @pl.when((i == last_i) & (f == MOE_NF - 1))
    def _():
        gather_block_wait(other)
        scatter_block_wait(other)


def _experts(blk_expert, src_ext, dst_ext, u, wg, wu, wd, *, n_out):
    nb = src_ext.shape[0] - 2
    assert wg.shape[2] == MOE_NF * MOE_TF and dst_ext.shape[0] == nb + 1
    idx = lambda fn: pl.BlockSpec((None, 1, MOE_BM), fn, memory_space=pltpu.SMEM)
    grid_spec = pltpu.PrefetchScalarGridSpec(
        num_scalar_prefetch=1,
        grid=(nb + 1, MOE_NF),
        in_specs=[idx(lambda i, f, be: (0, 0, 0)),
                  idx(lambda i, f, be: (i + 1, 0, 0)),
                  idx(lambda i, f, be: (i, 0, 0)),
                  pl.BlockSpec(memory_space=pl.ANY),
                  pl.BlockSpec((None, D_MODEL, MOE_TF), lambda i, f, be: (be[i], 0, f)),
                  pl.BlockSpec((None, D_MODEL, MOE_TF), lambda i, f, be: (be[i], 0, f)),
                  pl.BlockSpec((None, MOE_TF, D_MODEL), lambda i, f, be: (be[i], f, 0))],
        out_specs=pl.BlockSpec(memory_space=pl.ANY),
        scratch_shapes=[pltpu.VMEM((2, MOE_BM, D_MODEL), F32),
                        pltpu.VMEM((MOE_BM, D_MODEL), BF16),
                        pltpu.VMEM((2, MOE_BM, D_MODEL), F32),
                        pltpu.SemaphoreType.DMA((2,)),
                        pltpu.SemaphoreType.DMA((2,))],
    )
    return pl.pallas_call(
        _expert_kernel,
        grid_spec=grid_spec,
        out_shape=jax.ShapeDtypeStruct((n_out, D_MODEL), F32),
        compiler_params=_cparams(("arbitrary", "arbitrary"), 52 << 20),
        name="moe_experts",
    )(blk_expert, src_ext, src_ext, dst_ext, u, wg, wu, wd)


def _moe_tables(dest, t_tok, n_blocks):
    npad = n_blocks * MOE_BM
    pos = jnp.arange(npad, dtype=jnp.int32)
    trash = 2 * t_tok + ((pos // MOE_BM) % 2) * MOE_BM + pos % MOE_BM
    flat = dest.reshape(-1)
    asg = jnp.arange(2 * t_tok, dtype=jnp.int32)
    src = jnp.zeros((npad,), jnp.int32).at[flat].set(asg // 2)
    dst = trash.at[flat].set(asg)
    src_ext = jnp.concatenate([src, jnp.zeros((2 * MOE_BM,), jnp.int32)]).reshape(n_blocks + 2, 1, MOE_BM)
    lead = 2 * t_tok + MOE_BM + jnp.arange(MOE_BM, dtype=jnp.int32)
    dst_ext = jnp.concatenate([lead, dst]).reshape(n_blocks + 1, 1, MOE_BM)
    return src_ext, dst_ext


def _combine_kernel(y_ref, x_ref, rt_ref, gate_ref, g_ref, beta_ref, o_ref, *, spb):
    r = 1 + pl.program_id(0) // spb
    rt = rt_ref[...]
    y = rt[:, 2:3] * y_ref[:, :D_MODEL] + rt[:, 3:4] * y_ref[:, D_MODEL:]
    gate = gate_ref[pl.ds(r, 1), :]
    o_ref[...] = _layer_norm(ALPHA * x_ref[...] + gate * y, g_ref[...], beta_ref[...])


def _combine(y2, x1, rt, modt, g, beta, *, nblk, spb):
    kern = functools.partial(_combine_kernel, spb=spb)
    return pl.pallas_call(
        kern,
        grid=(nblk,),
        in_specs=[pl.BlockSpec((TM, 2 * D_MODEL), lambda j: (j, 0)),
                  pl.BlockSpec((TM, D_MODEL), lambda j: (j, 0)),
                  pl.BlockSpec((TM, LANES), lambda j: (j, 0)),
                  _mod_spec(1, 5, 1),
                  pl.BlockSpec((1, D_MODEL), lambda j: (0, 0)),
                  pl.BlockSpec((1, D_MODEL), lambda j: (0, 0))],
        out_specs=pl.BlockSpec((TM, D_MODEL), lambda j: (j, 0)),
        out_shape=jax.ShapeDtypeStruct((nblk * TM, D_MODEL), F32),
        compiler_params=_cparams(("arbitrary",), 24 << 20),
        name="moe_combine_ln",
    )(y2, x1, rt, modt, g, beta)


def _head_perm(partner, heads, width):
    return np.concatenate([h * width + partner for h in range(heads)])


def _even_weights(w_in, q_gain, k_gain, partner):
    perm_q = _head_perm(partner, A_Q_HEADS, HEAD_DIM)
    perm_k = _head_perm(partner, A_KV_HEADS, HEAD_DIM)
    wq = w_in[:, :A_Q_W]
    wk = w_in[:, A_Q_W:A_Q_W + A_KV_W]
    w_ext = jnp.concatenate([w_in, wq[:, perm_q], wk[:, perm_k]], axis=1).astype(BF16)
    gvec = jnp.zeros((8, LANES), F32)
    gvec = gvec.at[0].set(jnp.tile(q_gain, 2)).at[1].set(jnp.tile(q_gain[partner], 2))
    gvec = gvec.at[2].set(jnp.tile(k_gain, 2)).at[3].set(jnp.tile(k_gain[partner], 2))
    return w_ext, gvec


def _block_diag(blocks):
    n = blocks.shape[0]
    w = blocks.shape[1]
    out = jnp.zeros((n * w, n * w), blocks.dtype)
    for g in range(n):
        out = out.at[g * w:(g + 1) * w, g * w:(g + 1) * w].set(blocks[g])
    return out


def _odd_weights(w_in, w_q_up, w_kv_up, partner):
    o_kr = C_Q_RANK + C_KV_RANK
    w_kr = w_in[:, o_kr:o_kr + C_ROPE]
    w_rest = jnp.concatenate([w_in[:, :o_kr], w_in[:, o_kr + C_ROPE:]], axis=1)
    kr_blk = jnp.concatenate([w_kr, w_kr[:, partner], jnp.zeros((D_MODEL, LANES - 2 * C_ROPE), F32)], axis=1)
    w_ext = jnp.concatenate([w_rest, kr_blk], axis=1).astype(BF16)
    dq = C_NOPE + C_ROPE
    wq3 = w_q_up.reshape(C_Q_RANK, C_HEADS, dq)
    zq = jnp.zeros((C_Q_RANK, C_HEADS, LANES - dq), F32)
    wq_pad = jnp.concatenate([wq3, zq], axis=-1).reshape(C_Q_RANK, C_HEADS * LANES)
    wq_rope_p = wq3[:, :, C_NOPE:][:, :, partner]
    wq_padp = jnp.concatenate([jnp.zeros((C_Q_RANK, C_HEADS, C_NOPE), F32), wq_rope_p, zq], axis=-1)
    wq_padp = wq_padp.reshape(C_Q_RANK, C_HEADS * LANES)
    wkv3 = w_kv_up.reshape(C_KV_RANK, C_HEADS, C_NOPE + C_V)
    wk_pad = jnp.concatenate([wkv3[:, :, :C_NOPE], jnp.zeros((C_KV_RANK, C_HEADS, LANES - C_NOPE), F32)], axis=-1)
    wk_pad = wk_pad.reshape(C_KV_RANK, C_HEADS * LANES)
    wv = wkv3[:, :, C_NOPE:].reshape(C_KV_RANK, C_HEADS * C_V)
    pl2 = np.zeros((LANES, C_HEADS * LANES), np.float32)
    for h in range(C_HEADS):
        for dd in range(C_ROPE):
            pl2[dd, h * LANES + C_NOPE + dd] = 1.0
            pl2[C_ROPE + dd, h * LANES + C_NOPE + dd] = 1.0
    return (w_ext, wq_pad.astype(BF16), wq_padp.astype(BF16), wk_pad.astype(BF16), wv.astype(BF16),
            jnp.asarray(pl2, BF16))


def kernel(x, c, ctx, c_ctx, w_ada, b_ada, ln1_g, ln1_b, ln2_g, ln2_b,
           ev_w_in, ev_w_out, ev_q_gain, ev_k_gain, ev_w_pool, ev_pool_scale,
           ev_w_gate, ev_w_up, ev_w_down,
           od_w_in, od_w_out, od_q_lat_gain, od_kv_lat_gain, od_w_q_up, od_w_kv_up, od_na_bias,
           od_w_router, od_b_router, od_w_gate, od_w_up, od_w_down):
    batch, seq, d = x.shape
    ctx_len = ctx.shape[1]
    assert d == D_MODEL and ctx_len == TM and seq % TM == 0 and seq % GRID_W == 0
    assert seq // GRID_W >= NA_WIN_H and batch + 1 <= 8
    nt = ctx_len + seq
    nbpb = nt // TM
    nblk = batch * nbpb
    spb = seq // TM
    nxb = batch * spb

    xa = jnp.concatenate([ctx, x], axis=1).reshape(batch * nt, d)
    cin = jnp.zeros((8, d), F32).at[0].set(c_ctx).at[1:1 + batch].set(c)
    modt = _ada_table(cin, w_ada, b_ada)

    cos64, sin64, partner64 = _rope_axis_tables(HEAD_DIM, seq, ctx_len)
    w_ext0, gvec = _even_weights(ev_w_in[0], ev_q_gain[0], ev_k_gain[0], partner64)
    cos2 = jnp.tile(cos64, (1, 2))
    sin2 = jnp.tile(sin64, (1, 2))
    ones_bd = _block_diag(jnp.ones((2, HEAD_DIM, HEAD_DIM), BF16))
    qt, kk, vt, pp = _even_inproj(xa, modt, w_ext0, cos2, sin2, gvec, ones_bd, batch=batch, nt=nt)
    attn = _flash(qt, kk.reshape(batch, nt, A_KV_W), vt, batch=batch, nt=nt, units=A_KV_HEADS // 2,
                  nq=2 * A_GROUP, kw=LANES, k_off=(0,) * (2 * A_GROUP),
                  v_off=tuple((jq // A_GROUP) * VROWS for jq in range(2 * A_GROUP)),
                  ctx_queries=True, name="gqa_flash")
    pooled = _pool(pp, _block_diag(ev_w_pool[0]).astype(BF16), ev_pool_scale[0].reshape(1, POOL_WIDTH),
                   batch=batch, nt=nt)
    w_out0 = ev_w_out[0].astype(BF16)
    x1 = _proj_ln(attn.reshape(batch * nt, A_Q_W), pooled, w_out0[:A_Q_W], w_out0[A_Q_W:], xa, modt,
                  ln1_g[0].reshape(1, d), ln1_b[0].reshape(1, d), layer=0, nblk=nblk,
                  x_map=lambda j: (j, 0), row_fn=lambda j: _mod_row_all(j, nbpb), name="even_out_ln")
    xa1 = _ffn_ln(x1, modt, ev_w_gate[0].astype(BF16), ev_w_up[0].astype(BF16), ev_w_down[0].astype(BF16),
                  ln2_g[0].reshape(1, d), ln2_b[0].reshape(1, d), nbpb=nbpb, nblk=nblk)

    cos32, sin32, partner32 = _rope_axis_tables(C_ROPE, seq, ctx_len)
    w_ext1, wq_pad, wq_padp, wk_pad, wv, pl2 = _odd_weights(od_w_in[0], od_w_q_up[0], od_w_kv_up[0], partner32)
    ntr = cos32.shape[0]
    pad_tail = jnp.zeros((ntr, LANES - C_NOPE - C_ROPE), F32)
    cos_q = jnp.concatenate([jnp.ones((ntr, C_NOPE), F32), cos32, pad_tail], axis=1)
    sin_q = jnp.concatenate([jnp.zeros((ntr, C_NOPE), F32), sin32, pad_tail], axis=1)
    cs_k = jnp.concatenate([cos32, sin32, jnp.zeros((ntr, LANES - 2 * C_ROPE), F32)], axis=1)
    qt1, k1, vt1, nq, nk, nv = _odd_inproj(
        xa1, modt, w_ext1, od_q_lat_gain[0].reshape(1, C_Q_RANK), od_kv_lat_gain[0].reshape(1, C_KV_RANK),
        wq_pad, wq_padp, wk_pad, wv, pl2, cos_q, sin_q, cs_k, batch=batch, nt=nt)
    mla = _flash(qt1, k1.reshape(batch, nt, C_HEADS * LANES), vt1, batch=batch, nt=nt, units=C_HEADS // 2,
                 nq=2, kw=2 * LANES, k_off=(0, LANES), v_off=(0, VROWS), ctx_queries=False, name="mla_flash")
    na = _na(nq, nk, nv, _na_bias_table(od_na_bias[0]), batch=batch, nt=nt, seq=seq)
    w_out1 = od_w_out[0].astype(BF16)
    x1b = _proj_ln(mla.reshape(batch * seq, C_HEADS * C_V), na.reshape(batch * seq, NA_W),
                   w_out1[:C_HEADS * C_V], w_out1[C_HEADS * C_V:], xa1, modt,
                   ln1_g[1].reshape(1, d), ln1_b[1].reshape(1, d), layer=1, nblk=nxb,
                   x_map=lambda j: (j + j // spb + 1, 0), row_fn=lambda j: 1 + j // spb, name="odd_out_ln")

    wr = jnp.zeros((d, LANES), F32).at[:, :N_EXPERTS].set(od_w_router[0])
    wr_hi = wr.astype(BF16)
    wr_lo = (wr - wr_hi.astype(F32)).astype(BF16)
    br = jnp.full((1, LANES), NEG, F32).at[0, :N_EXPERTS].set(od_b_router[0])
    tri = jnp.asarray(np.tril(np.ones((TM, TM), np.float32), -1), BF16)
    rt, cnt, u_moe = _router(x1b, modt, wr_hi, wr_lo, br, tri, nblk=nxb, spb=spb)

    t_tok = batch * seq
    counts = cnt[0, :N_EXPERTS].astype(jnp.int32)
    padded = (counts + MOE_BM - 1) // MOE_BM * MOE_BM
    pad_end = jnp.cumsum(padded)
    pad_start = pad_end - padded
    n_blocks = -(-2 * t_tok // MOE_BM) + N_EXPERTS
    e12 = rt[:, 0:2].astype(jnp.int32)
    r12 = rt[:, 4:6].astype(jnp.int32)
    dest = pad_start[e12] + r12
    blk_start = jnp.arange(n_blocks + 1, dtype=jnp.int32) * MOE_BM
    blk_expert = jnp.minimum(jnp.sum((pad_end[None, :] <= blk_start[:, None]).astype(jnp.int32), axis=1),
                             N_EXPERTS - 1)
    src_ext, dst_ext = _moe_tables(dest, t_tok, n_blocks)

    y2 = _experts(blk_expert, src_ext, dst_ext, u_moe, od_w_gate[0].astype(BF16), od_w_up[0].astype(BF16),
                  od_w_down[0].astype(BF16), n_out=2 * t_tok + 2 * MOE_BM)
    out = _combine(y2.reshape(t_tok + MOE_BM, 2 * d), x1b, rt, modt, ln2_g[1].reshape(1, d),
                   ln2_b[1].reshape(1, d), nblk=nxb, spb=spb)
    return out.reshape(batch, seq, d)
```

```python
import functools
import math

import numpy as np
import jax
import jax.numpy as jnp
from jax import lax
from jax.experimental import pallas as pl
from jax.experimental.pallas import tpu as pltpu

F32 = jnp.float32
BF16 = jnp.bfloat16

D_MODEL = 1024
DEPTH = 2
GRID_W = 64
ROPE_THETA = 10000.0
NORM_EPS = 1e-6
ALPHA = (2 * DEPTH) ** 0.25
HEAD_DIM = 64
A_Q_HEADS = 12
A_KV_HEADS = 4
A_GROUP = A_Q_HEADS // A_KV_HEADS
POOL_WINDOWS = (2, 4, 8, 16)
POOL_GROUP = 64
POOL_WIDTH = POOL_GROUP * len(POOL_WINDOWS)
C_HEADS = 8
C_Q_RANK = 384
C_KV_RANK = 256
C_NOPE = 64
C_ROPE = 32
C_V = 64
D_HEADS = 8
NA_WIN_H = 8
NA_WIN_W = 16
D_FF = 2816
N_EXPERTS = 8
D_FF_EXPERT = 3584
A_Q_W = A_Q_HEADS * HEAD_DIM
A_KV_W = A_KV_HEADS * HEAD_DIM
NA_W = D_HEADS * HEAD_DIM

LANES = 128
TM = 256
VMEM_CAP = 56 * 1024 * 1024
MOE_BM = 512
MOE_TF = 1792
MOE_NF = D_FF_EXPERT // MOE_TF
VROWS = HEAD_DIM + 16
FLASH_STAGES = 96
FLASH_LOOKAHEAD = 6
NEG = -1e30
LOG2E = 1.4426950408889634


def _cparams(sem, vmem_bytes):
    return pltpu.CompilerParams(dimension_semantics=sem,
                                vmem_limit_bytes=int(min(VMEM_CAP, max(vmem_bytes, 16 * 1024 * 1024))))


def _dot(a, b):
    return jnp.dot(a, b, preferred_element_type=F32)


def _dot_nt(a, b):
    return lax.dot_general(a, b, (((1,), (1,)), ((), ())), preferred_element_type=F32)


def _layer_norm(z, g, b):
    mu = jnp.mean(z, axis=-1, keepdims=True)
    zc = z - mu
    var = jnp.mean(zc * zc, axis=-1, keepdims=True)
    return zc * lax.rsqrt(var + NORM_EPS) * g + b


def _ada_kernel(c_ref, w_ref, b_ref, o_ref):
    c = c_ref[...]
    s = c * jax.nn.sigmoid(c)
    o_ref[...] = _dot(s.astype(BF16), w_ref[...].astype(BF16)) + b_ref[...]


def _ada_table(cin, w_ada, b_ada):
    depth, d, n = w_ada.shape
    return pl.pallas_call(
        _ada_kernel,
        grid=(depth, n // d),
        in_specs=[pl.BlockSpec((8, d), lambda l, j: (0, 0)),
                  pl.BlockSpec((None, d, d), lambda l, j: (l, 0, j)),
                  pl.BlockSpec((None, 1, d), lambda l, j: (l, 0, j))],
        out_specs=pl.BlockSpec((None, 8, d), lambda l, j: (l, 0, j)),
        out_shape=jax.ShapeDtypeStruct((depth, 8, n), F32),
        compiler_params=_cparams(("arbitrary", "arbitrary"), 24 << 20),
        name="ada_table",
    )(cin, w_ada, b_ada.reshape(depth, 1, n))


def _mod_spec(layer, k, nargs):
    if nargs == 1:
        return pl.BlockSpec((None, 8, D_MODEL), lambda j: (layer, 0, k))
    return pl.BlockSpec((None, 8, D_MODEL), lambda *_: (layer, 0, k))


def _mod_row_all(j, nbpb):
    return jnp.where(j % nbpb == 0, 0, 1 + j // nbpb)


def _rope_axis_tables(rope_dims, seq, ctx):
    n = rope_dims // 2
    half = n // 2
    t = jnp.arange(seq, dtype=jnp.int32)
    row = (t // GRID_W).astype(F32)
    col = (t % GRID_W).astype(F32)
    inv = ROPE_THETA ** (-jnp.arange(half, dtype=F32) / half)
    cols_c, cols_s = [], []
    for pos in (row, col):
        ang = pos[:, None] * inv[None, :]
        c, s = jnp.cos(ang), jnp.sin(ang)
        cols_c += [c, c]
        cols_s += [-s, s]
    cos = jnp.concatenate(cols_c, axis=-1)
    sin = jnp.concatenate(cols_s, axis=-1)
    cos = jnp.concatenate([jnp.ones((ctx, rope_dims), F32), cos], axis=0)
    sin = jnp.concatenate([jnp.zeros((ctx, rope_dims), F32), sin], axis=0)
    d = np.arange(rope_dims)
    partner = (d // n) * n + ((d % n) + half) % n
    return cos, sin, partner


def _store_vt(vt_ref, vt, heads):
    ones = jnp.ones((VROWS - HEAD_DIM, TM), BF16)
    for h in range(heads):
        vt_ref[h * VROWS:h * VROWS + HEAD_DIM, :] = vt[h * HEAD_DIM:(h + 1) * HEAD_DIM, :]
        vt_ref[h * VROWS + HEAD_DIM:(h + 1) * VROWS, :] = ones


def _even_inproj_kernel(c_ref, x_ref, sh_ref, sc_ref, w_ref, cos_ref, sin_ref, gv_ref, ones_ref,
                        qt_ref, k_ref, vt_ref, p_ref, *, nbpb):
    j = pl.program_id(0)
    r = _mod_row_all(j, nbpb)
    shift = sh_ref[pl.ds(r, 1), :]
    scale = sc_ref[pl.ds(r, 1), :]
    u = _token_block(c_ref, x_ref, nbpb) * (1.0 + scale) + shift
    acc = _dot(u.astype(BF16), w_ref[...])
    cos = cos_ref[...]
    sin = sin_ref[...]
    ones_bd = ones_ref[...]

    def norm_rope(a, ap, g, gp):
        ssq = _dot((a * a).astype(BF16), ones_bd)
        rinv = lax.rsqrt(ssq * (1.0 / HEAD_DIM) + NORM_EPS)
        return (a * g * cos + ap * gp * sin) * rinv

    gq, gqp, gk, gkp = gv_ref[0:1, :], gv_ref[1:2, :], gv_ref[2:3, :], gv_ref[3:4, :]
    p_off = A_Q_W + 2 * A_KV_W + POOL_WIDTH
    zeros_half = jnp.zeros((HEAD_DIM, TM), BF16)
    for c in range(A_Q_W // LANES):
        a = acc[:, c * LANES:(c + 1) * LANES]
        ap = acc[:, p_off + c * LANES:p_off + (c + 1) * LANES]
        rot = norm_rope(a, ap, gq, gqp) * (HEAD_DIM ** -0.5 * LOG2E)
        rot_t = rot.T
        for hl in range(2):
            head = 2 * c + hl
            slot = (head // A_GROUP) % 2
            base = head * LANES
            qt_ref[base + slot * HEAD_DIM:base + (slot + 1) * HEAD_DIM, :] = (
                rot_t[hl * HEAD_DIM:(hl + 1) * HEAD_DIM, :].astype(BF16))
            qt_ref[base + (1 - slot) * HEAD_DIM:base + (2 - slot) * HEAD_DIM, :] = zeros_half
    for c in range(A_KV_W // LANES):
        a = acc[:, A_Q_W + c * LANES:A_Q_W + (c + 1) * LANES]
        ap = acc[:, p_off + A_Q_W + c * LANES:p_off + A_Q_W + (c + 1) * LANES]
        k_ref[:, c * LANES:(c + 1) * LANES] = norm_rope(a, ap, gk, gkp).astype(BF16)
    v = acc[:, A_Q_W + A_KV_W:A_Q_W + 2 * A_KV_W]
    _store_vt(vt_ref, v.T.astype(BF16), A_KV_HEADS)
    p_ref[...] = acc[:, A_Q_W + 2 * A_KV_W:A_Q_W + 2 * A_KV_W + POOL_WIDTH]


def _even_inproj(ctx2, x2, modt, w_ext, cos2, sin2, gvec, ones_bd, *, batch, nt):
    nbpb = nt // TM
    nblk = batch * nbpb
    n_ext = w_ext.shape[1]
    kern = functools.partial(_even_inproj_kernel, nbpb=nbpb)
    return pl.pallas_call(
        kern,
        grid=(nblk,),
        in_specs=[*_token_specs(nbpb),
                  _mod_spec(0, 0, 1), _mod_spec(0, 1, 1),
                  pl.BlockSpec((D_MODEL, n_ext), lambda j: (0, 0)),
                  pl.BlockSpec((TM, LANES), lambda j: (j % nbpb, 0)),
                  pl.BlockSpec((TM, LANES), lambda j: (j % nbpb, 0)),
                  pl.BlockSpec((8, LANES), lambda j: (0, 0)),
                  pl.BlockSpec((LANES, LANES), lambda j: (0, 0))],
        out_specs=[pl.BlockSpec((None, A_Q_HEADS * LANES, TM), lambda j: (j // nbpb, 0, j % nbpb)),
                   pl.BlockSpec((TM, A_KV_W), lambda j: (j, 0)),
                   pl.BlockSpec((None, None, A_KV_HEADS * VROWS, TM), lambda j: (j // nbpb, j % nbpb, 0, 0)),
                   pl.BlockSpec((TM, POOL_WIDTH), lambda j: (j, 0))],
        out_shape=[jax.ShapeDtypeStruct((batch, A_Q_HEADS * LANES, nt), BF16),
                   jax.ShapeDtypeStruct((batch * nt, A_KV_W), BF16),
                   jax.ShapeDtypeStruct((batch, nbpb, A_KV_HEADS * VROWS, TM), BF16),
                   jax.ShapeDtypeStruct((batch * nt, POOL_WIDTH), F32)],
        compiler_params=_cparams(("arbitrary",), 40 << 20),
        name="even_inproj",
    )(ctx2, x2, modt, modt, w_ext, cos2, sin2, gvec, ones_bd)


def _flash_kernel(q_ref, k_ref, v_ref, o_ref, acc_ref, m_ref, out_ref, *, nq, k_off, v_off, ctx_queries, nsub):
    nch = v_ref.shape[0]
    acc_ref[...] = jnp.zeros_like(acc_ref)
    m_ref[...] = jnp.full_like(m_ref, NEG)

    def scores(j, c):
        start = pl.multiple_of(c * TM, TM)
        kc = k_ref[pl.ds(start, TM), k_off[j]:k_off[j] + LANES]
        return _dot(kc, q_ref[j * LANES:(j + 1) * LANES, :])

    def update(j, c, s):
        row = slice(j, j + 1)
        m_old = m_ref[row, :]
        m_new = jnp.maximum(m_old, jnp.max(s, axis=0, keepdims=True))
        alpha = jnp.exp2(m_old - m_new)
        p = jnp.exp2(s - m_new)
        m_ref[row, :] = m_new
        pv = _dot(v_ref[c, v_off[j]:v_off[j] + VROWS, :], p.astype(BF16))
        rows = slice(j * VROWS, (j + 1) * VROWS)
        acc_ref[rows, :] = alpha * acc_ref[rows, :] + pv

    def run(chunks):
        stages = [(j, c) for c in chunks for j in range(nq)]
        ahead = [scores(*st) for st in stages[:FLASH_LOOKAHEAD]]
        for i, (j, c) in enumerate(stages):
            if i + FLASH_LOOKAHEAD < len(stages):
                ahead.append(scores(*stages[i + FLASH_LOOKAHEAD]))
            update(j, c, ahead.pop(0))

    run([0])
    n_main = (nch - 1) // nsub
    if ctx_queries:
        n_main = jnp.where(pl.program_id(2) == 0, 0, n_main)

    def body(it, carry):
        run([1 + it * nsub + t for t in range(nsub)])
        return carry

    lax.fori_loop(0, n_main, body, 0)
    for j in range(nq):
        denom = acc_ref[j * VROWS + HEAD_DIM:j * VROWS + HEAD_DIM + 1, :]
        out_ref[j * HEAD_DIM:(j + 1) * HEAD_DIM, :] = acc_ref[j * VROWS:j * VROWS + HEAD_DIM, :] / denom
    o_ref[...] = out_ref[...].T.astype(BF16)


def _flash(qt, k, vt, *, batch, nt, units, nq, kw, k_off, v_off, ctx_queries, name):
    nbpb = nt // TM
    nqb = nbpb if ctx_queries else nbpb - 1
    qoff = 0 if ctx_queries else 1
    nv = vt.shape[2] // units
    nsub = max(d for d in range(1, FLASH_STAGES // nq + 1) if (nbpb - 1) % d == 0)
    kern = functools.partial(_flash_kernel, nq=nq, k_off=k_off, v_off=v_off, ctx_queries=ctx_queries, nsub=nsub)
    vmem = 2 * (nt * kw * 2 + nt * nv * 2) + 4 * nq * LANES * TM * 2 + 3 * nq * HEAD_DIM * TM * 4 + (8 << 20)
    return pl.pallas_call(
        kern,
        grid=(batch, units, nqb),
        in_specs=[pl.BlockSpec((None, nq * LANES, TM), lambda b, u, i: (b, u, i + qoff)),
                  pl.BlockSpec((None, nt, kw), lambda b, u, i: (b, 0, u)),
                  pl.BlockSpec((None, nbpb, nv, TM), lambda b, u, i: (b, 0, u, 0))],
        out_specs=pl.BlockSpec((None, TM, nq * HEAD_DIM), lambda b, u, i: (b, i, u)),
        out_shape=jax.ShapeDtypeStruct((batch, nqb * TM, units * nq * HEAD_DIM), BF16),
        scratch_shapes=[pltpu.VMEM((nq * VROWS, TM), F32),
                        pltpu.VMEM((8, TM), F32),
                        pltpu.VMEM((nq * HEAD_DIM, TM), F32)],
        compiler_params=_cparams(("arbitrary", "arbitrary", "arbitrary"), vmem),
        name=name,
    )(qt, k, vt)


def _pool_kernel(prev_ref, cur_ref, next_ref, w_ref, ps_ref, o_ref, *, nbpb, nt):
    jj = pl.program_id(0) % nbpb
    base = jj * TM
    seg_lo = jnp.where(jj == 0, 0, TM)
    seg_hi = jnp.where(jj == 0, TM, nt)
    cur = cur_ref[...]
    xcat = jnp.concatenate([prev_ref[...], cur, next_ref[...]], axis=0)
    x_hi = xcat.astype(BF16)
    x_lo = (xcat - x_hi.astype(F32)).astype(BF16)
    t = base + lax.broadcasted_iota(jnp.int32, (TM, 3 * TM), 0)
    s = base - TM + lax.broadcasted_iota(jnp.int32, (TM, 3 * TM), 1)
    t1 = base + lax.broadcasted_iota(jnp.int32, (TM, 1), 0)
    lane_grp = lax.broadcasted_iota(jnp.int32, (TM, POOL_WIDTH), 1) // POOL_GROUP
    mean = jnp.zeros((TM, POOL_WIDTH), F32)
    for g, win in enumerate(POOL_WINDOWS):
        lo = win // 2
        hi = win - lo
        valid = ((s >= jnp.maximum(t - lo, seg_lo)) & (s < jnp.minimum(t + hi, seg_hi)))
        vb = valid.astype(F32).astype(BF16)
        wsum = _dot(vb, x_hi) + _dot(vb, x_lo)
        cnt = (jnp.minimum(t1 + hi, seg_hi) - jnp.maximum(t1 - lo, seg_lo)).astype(F32)
        mean = jnp.where(lane_grp == g, wsum / cnt, mean)
    y = (mean - cur).astype(BF16)
    o_ref[...] = (_dot(y, w_ref[...]) * ps_ref[...]).astype(BF16)


def _pool(p, w_bd, ps, *, batch, nt):
    nbpb = nt // TM
    nblk = batch * nbpb
    kern = functools.partial(_pool_kernel, nbpb=nbpb, nt=nt)

    def prev_map(j):
        return (jnp.maximum(j - 1, 0), 0)

    def next_map(j):
        return (jnp.minimum(j + 1, nblk - 1), 0)

    return pl.pallas_call(
        kern,
        grid=(nblk,),
        in_specs=[pl.BlockSpec((TM, POOL_WIDTH), prev_map),
                  pl.BlockSpec((TM, POOL_WIDTH), lambda j: (j, 0)),
                  pl.BlockSpec((TM, POOL_WIDTH), next_map),
                  pl.BlockSpec((POOL_WIDTH, POOL_WIDTH), lambda j: (0, 0)),
                  pl.BlockSpec((1, POOL_WIDTH), lambda j: (0, 0))],
        out_specs=pl.BlockSpec((TM, POOL_WIDTH), lambda j: (j, 0)),
        out_shape=jax.ShapeDtypeStruct((batch * nt, POOL_WIDTH), BF16),
        compiler_params=_cparams(("arbitrary",), 24 << 20),
        name="pool_mixer",
    )(p, p, p, w_bd, ps)


def _token_specs(nbpb):
    spb = nbpb - 1
    return [pl.BlockSpec((TM, D_MODEL), lambda j: (j // nbpb, 0)),
            pl.BlockSpec((TM, D_MODEL), lambda j: ((j // nbpb) * spb + jnp.maximum(j % nbpb - 1, 0), 0))]


def _token_block(c_ref, x_ref, nbpb):
    return jnp.where(pl.program_id(0) % nbpb == 0, c_ref[...], x_ref[...])


def _proj_ln_kernel(a_ref, b_ref, wa_ref, wb_ref, *rest, row_fn, nbpb):
    if nbpb:
        c_ref, x_ref, gate_ref, g_ref, beta_ref, o_ref = rest
        x = _token_block(c_ref, x_ref, nbpb)
    else:
        x_ref, gate_ref, g_ref, beta_ref, o_ref = rest
        x = x_ref[...]
    r = row_fn(pl.program_id(0))
    gate = gate_ref[pl.ds(r, 1), :]
    y = _dot(a_ref[...], wa_ref[...]) + _dot(b_ref[...], wb_ref[...])
    o_ref[...] = _layer_norm(ALPHA * x + gate * y, g_ref[...], beta_ref[...])


def _proj_ln(a, b, wa, wb, xs, modt, g, beta, *, layer, nblk, x_specs, row_fn, name, nbpb=0):
    ka, kb = a.shape[1], b.shape[1]
    kern = functools.partial(_proj_ln_kernel, row_fn=row_fn, nbpb=nbpb)
    return pl.pallas_call(
        kern,
        grid=(nblk,),
        in_specs=[pl.BlockSpec((TM, ka), lambda j: (j, 0)),
                  pl.BlockSpec((TM, kb), lambda j: (j, 0)),
                  pl.BlockSpec((ka, D_MODEL), lambda j: (0, 0)),
                  pl.BlockSpec((kb, D_MODEL), lambda j: (0, 0)),
                  *x_specs,
                  _mod_spec(layer, 2, 1),
                  pl.BlockSpec((1, D_MODEL), lambda j: (0, 0)),
                  pl.BlockSpec((1, D_MODEL), lambda j: (0, 0))],
        out_specs=pl.BlockSpec((TM, D_MODEL), lambda j: (j, 0)),
        out_shape=jax.ShapeDtypeStruct((nblk * TM, D_MODEL), F32),
        compiler_params=_cparams(("arbitrary",), 32 << 20),
        name=name,
    )(a, b, wa, wb, *xs, modt, g, beta)


def _ffn_ln_kernel(x_ref, sh_ref, sc_ref, gate_ref, wg_ref, wu_ref, wd_ref, g_ref, beta_ref, o_ref, *, nbpb):
    r = _mod_row_all(pl.program_id(0), nbpb)
    x = x_ref[...]
    u = (x * (1.0 + sc_ref[pl.ds(r, 1), :]) + sh_ref[pl.ds(r, 1), :]).astype(BF16)
    hg = _dot(u, wg_ref[...])
    hu = _dot(u, wu_ref[...])
    h = (hg * jax.nn.sigmoid(hg) * hu).astype(BF16)
    y = _dot(h, wd_ref[...])
    gate = gate_ref[pl.ds(r, 1), :]
    o_ref[...] = _layer_norm(ALPHA * x + gate * y, g_ref[...], beta_ref[...])


def _resident(shape):
    return pl.BlockSpec(shape, lambda *_: (0,) * len(shape), pipeline_mode=pl.Buffered(1))


def _ffn_ln(x1, modt, wg, wu, wd, g, beta, *, nbpb, nblk):
    dff = wg.shape[1]
    kern = functools.partial(_ffn_ln_kernel, nbpb=nbpb)
    return pl.pallas_call(
        kern,
        grid=(nblk,),
        in_specs=[pl.BlockSpec((TM, D_MODEL), lambda j: (j, 0)),
                  _mod_spec(0, 3, 1), _mod_spec(0, 4, 1), _mod_spec(0, 5, 1),
                  _resident((D_MODEL, dff)), _resident((D_MODEL, dff)), _resident((dff, D_MODEL)),
                  pl.BlockSpec((1, D_MODEL), lambda j: (0, 0)),
                  pl.BlockSpec((1, D_MODEL), lambda j: (0, 0))],
        out_specs=pl.BlockSpec((TM, D_MODEL), lambda j: (j, 0)),
        out_shape=jax.ShapeDtypeStruct((nblk * TM, D_MODEL), F32),
        compiler_params=_cparams(("arbitrary",), 44 << 20),
        name="even_ffn_ln",
    )(x1, modt, modt, modt, wg, wu, wd, g, beta)


def _odd_inproj_kernel(x_ref, sh_ref, sc_ref, w_ref, gq_ref, gkv_ref, wq_ref, wqp_ref, wk_ref, wv_ref,
                       pl2_ref, cq_ref, sq_ref, cs_ref,
                       qt_ref, k_ref, vt_ref, nq_ref, nk_ref, nv_ref, *, nbpb):
    j = pl.program_id(0)
    r = _mod_row_all(j, nbpb)
    u = x_ref[...] * (1.0 + sc_ref[pl.ds(r, 1), :]) + sh_ref[pl.ds(r, 1), :]
    acc = _dot(u.astype(BF16), w_ref[...])
    o = 0
    cq = acc[:, o:o + C_Q_RANK]; o += C_Q_RANK
    ckv = acc[:, o:o + C_KV_RANK]; o += C_KV_RANK
    nq = acc[:, o:o + NA_W]; o += NA_W
    nk = acc[:, o:o + NA_W]; o += NA_W
    nv = acc[:, o:o + NA_W]; o += NA_W
    krc = acc[:, o:o + LANES]

    def rms(a, g):
        return (a * lax.rsqrt(jnp.mean(a * a, axis=-1, keepdims=True) + NORM_EPS) * g).astype(BF16)

    cqn = rms(cq, gq_ref[...])
    kvn = rms(ckv, gkv_ref[...])
    cos_q = jnp.tile(cq_ref[...], (1, C_HEADS))
    sin_q = jnp.tile(sq_ref[...], (1, C_HEADS))
    q = _dot(cqn, wq_ref[...]) * cos_q + _dot(cqn, wqp_ref[...]) * sin_q
    q = q * ((C_NOPE + C_ROPE) ** -0.5 * LOG2E)
    qt_ref[...] = q.T.astype(BF16)
    kr_terms = (krc * cs_ref[...]).astype(BF16)
    k_ref[...] = (_dot(kvn, wk_ref[...]) + _dot(kr_terms, pl2_ref[...])).astype(BF16)
    _store_vt(vt_ref, _dot(kvn, wv_ref[...]).T.astype(BF16), C_HEADS)
    nq_ref[...] = (nq * HEAD_DIM ** -0.5).astype(BF16)
    nk_ref[...] = nk.astype(BF16)
    nv_ref[...] = nv.astype(BF16)


def _odd_inproj(xa, modt, w_ext, gq, gkv, wq_pad, wq_padp, wk_pad, wv, pl2, cos_q, sin_q, cs_k, *, batch, nt):
    nbpb = nt // TM
    nblk = batch * nbpb
    n_ext = w_ext.shape[1]
    hq = C_HEADS * LANES
    hv = C_HEADS * C_V
    kern = functools.partial(_odd_inproj_kernel, nbpb=nbpb)
    const = lambda j: (0, 0)
    tok = lambda j: (j, 0)
    pos = lambda j: (j % nbpb, 0)
    return pl.pallas_call(
        kern,
        grid=(nblk,),
        in_specs=[pl.BlockSpec((TM, D_MODEL), tok),
                  _mod_spec(1, 0, 1), _mod_spec(1, 1, 1),
                  pl.BlockSpec((D_MODEL, n_ext), const),
                  pl.BlockSpec((1, C_Q_RANK), const),
                  pl.BlockSpec((1, C_KV_RANK), const),
                  pl.BlockSpec((C_Q_RANK, hq), const),
                  pl.BlockSpec((C_Q_RANK, hq), const),
                  pl.BlockSpec((C_KV_RANK, hq), const),
                  pl.BlockSpec((C_KV_RANK, hv), const),
                  pl.BlockSpec((LANES, hq), const),
                  pl.BlockSpec((TM, LANES), pos),
                  pl.BlockSpec((TM, LANES), pos),
                  pl.BlockSpec((TM, LANES), pos)],
        out_specs=[pl.BlockSpec((None, hq, TM), lambda j: (j // nbpb, 0, j % nbpb)),
                   pl.BlockSpec((TM, hq), tok),
                   pl.BlockSpec((None, None, C_HEADS * VROWS, TM), lambda j: (j // nbpb, j % nbpb, 0, 0)),
                   pl.BlockSpec((TM, NA_W), tok),
                   pl.BlockSpec((TM, NA_W), tok),
                   pl.BlockSpec((TM, NA_W), tok)],
        out_shape=[jax.ShapeDtypeStruct((batch, hq, nt), BF16),
                   jax.ShapeDtypeStruct((batch * nt, hq), BF16),
                   jax.ShapeDtypeStruct((batch, nbpb, C_HEADS * VROWS, TM), BF16),
                   jax.ShapeDtypeStruct((batch * nt, NA_W), BF16),
                   jax.ShapeDtypeStruct((batch * nt, NA_W), BF16),
                   jax.ShapeDtypeStruct((batch * nt, NA_W), BF16)],
        compiler_params=_cparams(("arbitrary",), 48 << 20),
        name="odd_inproj",
    )(xa, modt, modt, w_ext, gq, gkv, wq_pad, wq_padp, wk_pad, wv, pl2, cos_q, sin_q, cs_k)


def _na_kernel(q_ref, kb_ref, vb_ref, kc_ref, vc_ref, bias_ref, o_ref):
    lane = lax.broadcasted_iota(jnp.int32, (GRID_W, LANES), 1)
    first_half = (lane // HEAD_DIM) == 0
    scores = []
    for h in range(D_HEADS):
        cols = slice((h // 2) * LANES, (h // 2 + 1) * LANES)
        qp = q_ref[:, cols]
        qh = jnp.where(first_half == (h % 2 == 0), qp, jnp.zeros_like(qp))
        s_w = _dot_nt(qh, kb_ref[0, :, cols]) + bias_ref[h]
        s_c = _dot_nt(qh, kc_ref[:, cols])
        scores.append((s_w, s_c))
    outs = []
    for h in range(D_HEADS):
        cols = slice((h // 2) * LANES, (h // 2 + 1) * LANES)
        s_w, s_c = scores[h]
        m = jnp.maximum(jnp.max(s_w, axis=-1, keepdims=True), jnp.max(s_c, axis=-1, keepdims=True))
        p_w = jnp.exp(s_w - m)
        p_c = jnp.exp(s_c - m)
        l = jnp.sum(p_w, axis=-1, keepdims=True) + jnp.sum(p_c, axis=-1, keepdims=True)
        o = _dot(p_w.astype(BF16), vb_ref[0, :, cols]) + _dot(p_c.astype(BF16), vc_ref[:, cols])
        outs.append(o / l)
    for hp in range(D_HEADS // 2):
        o_ref[:, hp * LANES:(hp + 1) * LANES] = jnp.where(first_half, outs[2 * hp], outs[2 * hp + 1]).astype(BF16)


def _na(nq, nk, nv, bias_tab, *, batch, nt, seq):
    rows = seq // GRID_W
    kh = NA_WIN_H
    band = kh * GRID_W
    ctx_rows = TM // GRID_W

    def rs_of(r):
        return jnp.clip(r - kh // 2, 0, rows - kh)

    nq3 = nq.reshape(batch, nt, NA_W)
    nk3 = nk.reshape(batch, nt, NA_W)
    nv3 = nv.reshape(batch, nt, NA_W)
    band_spec = pl.BlockSpec((pl.Element(1), pl.Element(band), pl.Element(NA_W)),
                             lambda b, r: (b, pl.multiple_of(TM + rs_of(r) * GRID_W, GRID_W), 0))
    ctx_spec = pl.BlockSpec((None, TM, NA_W), lambda b, r: (b, 0, 0))
    return pl.pallas_call(
        _na_kernel,
        grid=(batch, rows),
        in_specs=[pl.BlockSpec((None, GRID_W, NA_W), lambda b, r: (b, ctx_rows + r, 0)),
                  band_spec, band_spec, ctx_spec, ctx_spec,
                  pl.BlockSpec((None, D_HEADS, GRID_W, band), lambda b, r: (rs_of(r) - r + NA_WIN_H - 1, 0, 0, 0))],
        out_specs=pl.BlockSpec((None, GRID_W, NA_W), lambda b, r: (b, r, 0)),
        out_shape=jax.ShapeDtypeStruct((batch, seq, NA_W), BF16),
        compiler_params=_cparams(("arbitrary", "arbitrary"), 24 << 20),
        name="neighborhood_attn",
    )(nq3, nk3, nv3, nk3, nv3, bias_tab)


def _na_bias_table(rel_bias):
    j = np.arange(GRID_W)
    col_start = np.clip(j - NA_WIN_W // 2, 0, GRID_W - NA_WIN_W)
    col_valid = (j[None, :] >= col_start[:, None]) & (j[None, :] < col_start[:, None] + NA_WIN_W)
    col_idx = np.clip(j[None, :] - j[:, None] + NA_WIN_W - 1, 0, 2 * NA_WIN_W - 2)
    ds = np.arange(NA_WIN_H)[:, None] + np.arange(NA_WIN_H)[None, :]
    onehot = jnp.asarray(col_idx[:, :, None] == np.arange(2 * NA_WIN_W - 1), F32)
    tab = jnp.einsum('hdic,qkc->hdiqk', rel_bias[:, ds], onehot,
                     precision=lax.Precision.HIGHEST)
    tab = jnp.where(col_valid[None, None, None], tab, NEG)
    tab = tab.transpose(1, 0, 3, 2, 4)
    return tab.reshape(NA_WIN_H, D_HEADS, GRID_W, NA_WIN_H * GRID_W).astype(F32)


def _router_kernel(x_ref, sh_ref, sc_ref, wh_ref, wl_ref, b_ref, tri_ref, o_ref, cnt_ref, u_ref, base_ref, *, spb):
    j = pl.program_id(0)

    @pl.when(j == 0)
    def _():
        base_ref[...] = jnp.zeros_like(base_ref)

    r = 1 + j // spb
    u = x_ref[...] * (1.0 + sc_ref[pl.ds(r, 1), :]) + sh_ref[pl.ds(r, 1), :]
    u_ref[...] = u
    u_hi = u.astype(BF16)
    u_lo = (u - u_hi.astype(F32)).astype(BF16)
    wh = wh_ref[...]
    logits = _dot(u_hi, wh) + _dot(u_lo, wh) + _dot(u_hi, wl_ref[...]) + b_ref[...]
    lane = lax.broadcasted_iota(jnp.int32, (TM, LANES), 1)
    m1 = jnp.max(logits, axis=-1, keepdims=True)
    i1 = jnp.min(jnp.where(logits == m1, lane, LANES), axis=-1, keepdims=True)
    rest = jnp.where(lane == i1, NEG * 2, logits)
    m2 = jnp.max(rest, axis=-1, keepdims=True)
    i2 = jnp.min(jnp.where(rest == m2, lane, LANES), axis=-1, keepdims=True)
    e21 = jnp.exp(m2 - m1)
    g1 = 1.0 / (1.0 + e21)
    g2 = e21 / (1.0 + e21)
    oh1 = (lane == i1).astype(F32)
    oh2 = (lane == i2).astype(F32)
    both = oh1 + oh2
    before = _dot(tri_ref[...], both.astype(BF16)) + base_ref[0:1, :]
    r1 = jnp.sum(oh1 * before, axis=-1, keepdims=True)
    r2 = jnp.sum(oh2 * before, axis=-1, keepdims=True)
    out = jnp.zeros((TM, LANES), F32)
    for idx, col in enumerate((i1.astype(F32), i2.astype(F32), g1, g2, r1, r2)):
        out = jnp.where(lane == idx, col, out)
    o_ref[...] = out
    total = base_ref[0:1, :] + jnp.sum(both, axis=0, keepdims=True)
    base_ref[...] = jnp.broadcast_to(total, base_ref.shape)
    cnt_ref[...] = jnp.broadcast_to(total, cnt_ref.shape)


def _router(x1, modt, wr_hi, wr_lo, br, tri, *, nblk, spb):
    kern = functools.partial(_router_kernel, spb=spb)
    return pl.pallas_call(
        kern,
        grid=(nblk,),
        in_specs=[pl.BlockSpec((TM, D_MODEL), lambda j: (j, 0)),
                  _mod_spec(1, 3, 1), _mod_spec(1, 4, 1),
                  pl.BlockSpec((D_MODEL, LANES), lambda j: (0, 0)),
                  pl.BlockSpec((D_MODEL, LANES), lambda j: (0, 0)),
                  pl.BlockSpec((1, LANES), lambda j: (0, 0)),
                  pl.BlockSpec((TM, TM), lambda j: (0, 0))],
        out_specs=[pl.BlockSpec((TM, LANES), lambda j: (j, 0)),
                   pl.BlockSpec((8, LANES), lambda j: (0, 0)),
                   pl.BlockSpec((TM, D_MODEL), lambda j: (j, 0))],
        out_shape=[jax.ShapeDtypeStruct((nblk * TM, LANES), F32),
                   jax.ShapeDtypeStruct((8, LANES), F32),
                   jax.ShapeDtypeStruct((nblk * TM, D_MODEL), F32)],
        scratch_shapes=[pltpu.VMEM((8, LANES), F32)],
        compiler_params=_cparams(("arbitrary",), 24 << 20),
        name="moe_router",
    )(x1, modt, modt, wr_hi, wr_lo, br, tri)


def _expert_kernel(be_ref, src0_ref, srcn_ref, dstp_ref, u_ref, wg_ref, wu_ref, wd_ref, y_ref,
                   xbuf_ref, xb_ref, obuf_ref, gsem, ssem):
    del be_ref
    i = pl.program_id(0)
    f = pl.program_id(1)
    last_i = pl.num_programs(0) - 1
    slot = i % 2
    other = 1 - slot
    half = MOE_BM // MOE_NF

    def gather_row(src_ref, r, dst_slot):
        return pltpu.make_async_copy(u_ref.at[pl.ds(src_ref[0, r], 1)],
                                     xbuf_ref.at[dst_slot, pl.ds(r, 1)], gsem.at[dst_slot])

    def scatter_row(r, src_slot):
        return pltpu.make_async_copy(obuf_ref.at[src_slot, pl.ds(r, 1)],
                                     y_ref.at[pl.ds(dstp_ref[0, r], 1)], ssem.at[src_slot])

    def gather_block_wait(s):
        pltpu.make_async_copy(u_ref.at[pl.ds(0, MOE_BM)], xbuf_ref.at[s], gsem.at[s]).wait()

    def scatter_block_wait(s):
        pltpu.make_async_copy(obuf_ref.at[s], y_ref.at[pl.ds(0, MOE_BM)], ssem.at[s]).wait()

    @pl.when((i == 0) & (f == 0))
    def _():
        obuf_ref[...] = jnp.zeros_like(obuf_ref)

        def first(r, c):
            gather_row(src0_ref, r, 0).start()
            return c

        lax.fori_loop(0, MOE_BM, first, 0)

    @pl.when(f == 0)
    def _():
        gather_block_wait(slot)
        xb_ref[...] = xbuf_ref[slot].astype(BF16)

    @pl.when((f == 0) & (i >= 1))
    def _():
        scatter_block_wait(slot)

    for t in range(half):
        r = f * half + t
        gather_row(srcn_ref, r, other).start(priority=0)
        scatter_row(r, other).start(priority=1)

    xb = xb_ref[...]
    hg = _dot(xb, wg_ref[...])
    hu = _dot(xb, wu_ref[...])
    h = (hg * jax.nn.sigmoid(hg) * hu).astype(BF16)
    part = _dot(h, wd_ref[...])

    @pl.when(f == 0)
    def _():
        obuf_ref[slot] = part

    @pl.when(f > 0)
    def _():
        obuf_ref[slot] += part

    @pl.when((i == last_i) & (f == MOE_NF - 1))
    def _():
        gather_block_wait(other)
        scatter_block_wait(other)


def _experts(blk_expert, src_ext, dst_ext, u, wg, wu, wd, *, n_out):
    nb = src_ext.shape[0] - 2
    assert wg.shape[2] == MOE_NF * MOE_TF and dst_ext.shape[0] == nb + 1
    idx = lambda fn: pl.BlockSpec((None, 1, MOE_BM), fn, memory_space=pltpu.SMEM)
    grid_spec = pltpu.PrefetchScalarGridSpec(
        num_scalar_prefetch=1,
        grid=(nb + 1, MOE_NF),
        in_specs=[idx(lambda i, f, be: (0, 0, 0)),
                  idx(lambda i, f, be: (i + 1, 0, 0)),
                  idx(lambda i, f, be: (i, 0, 0)),
                  pl.BlockSpec(memory_space=pl.ANY),
                  pl.BlockSpec((None, D_MODEL, MOE_TF), lambda i, f, be: (be[i], 0, f)),
                  pl.BlockSpec((None, D_MODEL, MOE_TF), lambda i, f, be: (be[i], 0, f)),
                  pl.BlockSpec((None, MOE_TF, D_MODEL), lambda i, f, be: (be[i], f, 0))],
        out_specs=pl.BlockSpec(memory_space=pl.ANY),
        scratch_shapes=[pltpu.VMEM((2, MOE_BM, D_MODEL), F32),
                        pltpu.VMEM((MOE_BM, D_MODEL), BF16),
                        pltpu.VMEM((2, MOE_BM, D_MODEL), F32),
                        pltpu.SemaphoreType.DMA((2,)),
                        pltpu.SemaphoreType.DMA((2,))],
    )
    return pl.pallas_call(
        _expert_kernel,
        grid_spec=grid_spec,
        out_shape=jax.ShapeDtypeStruct((n_out, D_MODEL), F32),
        compiler_params=_cparams(("arbitrary", "arbitrary"), 52 << 20),
        name="moe_experts",
    )(blk_expert, src_ext, src_ext, dst_ext, u, wg, wu, wd)


def _moe_tables(dest, t_tok, n_blocks):
    npad = n_blocks * MOE_BM
    pos = jnp.arange(npad, dtype=jnp.int32)
    trash = 2 * t_tok + ((pos // MOE_BM) % 2) * MOE_BM + pos % MOE_BM
    flat = dest.reshape(-1)
    asg = jnp.arange(2 * t_tok, dtype=jnp.int32)
    tok, slot = asg // 2, asg % 2
    init = jnp.stack([jnp.zeros((npad,), jnp.int32), trash], axis=1)
    both = init.at[flat].set(jnp.stack([tok, slot * t_tok + tok], axis=1),
                             unique_indices=True, mode='promise_in_bounds')
    src, dst = both[:, 0], both[:, 1]
    src_ext = jnp.concatenate([src, jnp.zeros((2 * MOE_BM,), jnp.int32)]).reshape(n_blocks + 2, 1, MOE_BM)
    lead = 2 * t_tok + MOE_BM + jnp.arange(MOE_BM, dtype=jnp.int32)
    dst_ext = jnp.concatenate([lead, dst]).reshape(n_blocks + 1, 1, MOE_BM)
    return src_ext, dst_ext


def _combine_kernel(y0_ref, y1_ref, x_ref, rt_ref, gate_ref, g_ref, beta_ref, o_ref, *, spb):
    r = 1 + pl.program_id(0) // spb
    rt = rt_ref[...]
    y = rt[:, 2:3] * y0_ref[...] + rt[:, 3:4] * y1_ref[...]
    gate = gate_ref[pl.ds(r, 1), :]
    o_ref[...] = _layer_norm(ALPHA * x_ref[...] + gate * y, g_ref[...], beta_ref[...])


def _combine(y2, x1, rt, modt, g, beta, *, nblk, spb):
    kern = functools.partial(_combine_kernel, spb=spb)
    return pl.pallas_call(
        kern,
        grid=(nblk,),
        in_specs=[pl.BlockSpec((TM, D_MODEL), lambda j: (j, 0)),
                  pl.BlockSpec((TM, D_MODEL), lambda j: (j + nblk, 0)),
                  pl.BlockSpec((TM, D_MODEL), lambda j: (j, 0)),
                  pl.BlockSpec((TM, LANES), lambda j: (j, 0)),
                  _mod_spec(1, 5, 1),
                  pl.BlockSpec((1, D_MODEL), lambda j: (0, 0)),
                  pl.BlockSpec((1, D_MODEL), lambda j: (0, 0))],
        out_specs=pl.BlockSpec((TM, D_MODEL), lambda j: (j, 0)),
        out_shape=jax.ShapeDtypeStruct((nblk * TM, D_MODEL), F32),
        compiler_params=_cparams(("arbitrary",), 24 << 20),
        name="moe_combine_ln",
    )(y2, y2, x1, rt, modt, g, beta)


def _head_perm(partner, heads, width):
    return np.concatenate([h * width + partner for h in range(heads)])


def _even_weights(w_in, q_gain, k_gain, partner):
    perm_q = _head_perm(partner, A_Q_HEADS, HEAD_DIM)
    perm_k = _head_perm(partner, A_KV_HEADS, HEAD_DIM)
    wq = w_in[:, :A_Q_W]
    wk = w_in[:, A_Q_W:A_Q_W + A_KV_W]
    w_ext = jnp.concatenate([w_in, wq[:, perm_q], wk[:, perm_k]], axis=1).astype(BF16)
    gvec = jnp.zeros((8, LANES), F32)
    gvec = gvec.at[0].set(jnp.tile(q_gain, 2)).at[1].set(jnp.tile(q_gain[partner], 2))
    gvec = gvec.at[2].set(jnp.tile(k_gain, 2)).at[3].set(jnp.tile(k_gain[partner], 2))
    return w_ext, gvec


def _block_diag(blocks):
    n = blocks.shape[0]
    w = blocks.shape[1]
    out = jnp.zeros((n * w, n * w), blocks.dtype)
    for g in range(n):
        out = out.at[g * w:(g + 1) * w, g * w:(g + 1) * w].set(blocks[g])
    return out


def _odd_weights(w_in, w_q_up, w_kv_up, partner):
    o_kr = C_Q_RANK + C_KV_RANK
    w_kr = w_in[:, o_kr:o_kr + C_ROPE]
    w_rest = jnp.concatenate([w_in[:, :o_kr], w_in[:, o_kr + C_ROPE:]], axis=1)
    kr_blk = jnp.concatenate([w_kr, w_kr[:, partner], jnp.zeros((D_MODEL, LANES - 2 * C_ROPE), F32)], axis=1)
    w_ext = jnp.concatenate([w_rest, kr_blk], axis=1).astype(BF16)
    dq = C_NOPE + C_ROPE
    wq3 = w_q_up.reshape(C_Q_RANK, C_HEADS, dq)
    zq = jnp.zeros((C_Q_RANK, C_HEADS, LANES - dq), F32)
    wq_pad = jnp.concatenate([wq3, zq], axis=-1).reshape(C_Q_RANK, C_HEADS * LANES)
    wq_rope_p = wq3[:, :, C_NOPE:][:, :, partner]
    wq_padp = jnp.concatenate([jnp.zeros((C_Q_RANK, C_HEADS, C_NOPE), F32), wq_rope_p, zq], axis=-1)
    wq_padp = wq_padp.reshape(C_Q_RANK, C_HEADS * LANES)
    wkv3 = w_kv_up.reshape(C_KV_RANK, C_HEADS, C_NOPE + C_V)
    wk_pad = jnp.concatenate([wkv3[:, :, :C_NOPE], jnp.zeros((C_KV_RANK, C_HEADS, LANES - C_NOPE), F32)], axis=-1)
    wk_pad = wk_pad.reshape(C_KV_RANK, C_HEADS * LANES)
    wv = wkv3[:, :, C_NOPE:].reshape(C_KV_RANK, C_HEADS * C_V)
    pl2 = np.zeros((LANES, C_HEADS * LANES), np.float32)
    for h in range(C_HEADS):
        for dd in range(C_ROPE):
            pl2[dd, h * LANES + C_NOPE + dd] = 1.0
            pl2[C_ROPE + dd, h * LANES + C_NOPE + dd] = 1.0
    return (w_ext, wq_pad.astype(BF16), wq_padp.astype(BF16), wk_pad.astype(BF16), wv.astype(BF16),
            jnp.asarray(pl2, BF16))


def kernel(x, c, ctx, c_ctx, w_ada, b_ada, ln1_g, ln1_b, ln2_g, ln2_b,
           ev_w_in, ev_w_out, ev_q_gain, ev_k_gain, ev_w_pool, ev_pool_scale,
           ev_w_gate, ev_w_up, ev_w_down,
           od_w_in, od_w_out, od_q_lat_gain, od_kv_lat_gain, od_w_q_up, od_w_kv_up, od_na_bias,
           od_w_router, od_b_router, od_w_gate, od_w_up, od_w_down):
    batch, seq, d = x.shape
    ctx_len = ctx.shape[1]
    assert d == D_MODEL and ctx_len == TM and seq % TM == 0 and seq % GRID_W == 0
    assert seq // GRID_W >= NA_WIN_H and batch + 1 <= 8
    nt = ctx_len + seq
    nbpb = nt // TM
    nblk = batch * nbpb
    spb = seq // TM
    nxb = batch * spb

    ctx2 = ctx.reshape(batch * ctx_len, d)
    x2 = x.reshape(batch * seq, d)
    cin = jnp.zeros((8, d), F32).at[0].set(c_ctx).at[1:1 + batch].set(c)
    modt = _ada_table(cin, w_ada, b_ada)

    cos64, sin64, partner64 = _rope_axis_tables(HEAD_DIM, seq, ctx_len)
    w_ext0, gvec = _even_weights(ev_w_in[0], ev_q_gain[0], ev_k_gain[0], partner64)
    cos2 = jnp.tile(cos64, (1, 2))
    sin2 = jnp.tile(sin64, (1, 2))
    ones_bd = _block_diag(jnp.ones((2, HEAD_DIM, HEAD_DIM), BF16))
    qt, kk, vt, pp = _even_inproj(ctx2, x2, modt, w_ext0, cos2, sin2, gvec, ones_bd, batch=batch, nt=nt)
    attn = _flash(qt, kk.reshape(batch, nt, A_KV_W), vt, batch=batch, nt=nt, units=A_KV_HEADS // 2,
                  nq=2 * A_GROUP, kw=LANES, k_off=(0,) * (2 * A_GROUP),
                  v_off=tuple((jq // A_GROUP) * VROWS for jq in range(2 * A_GROUP)),
                  ctx_queries=True, name="gqa_flash")
    pooled = _pool(pp, _block_diag(ev_w_pool[0]).astype(BF16), ev_pool_scale[0].reshape(1, POOL_WIDTH),
                   batch=batch, nt=nt)
    w_out0 = ev_w_out[0].astype(BF16)
    x1 = _proj_ln(attn.reshape(batch * nt, A_Q_W), pooled, w_out0[:A_Q_W], w_out0[A_Q_W:], (ctx2, x2), modt,
                  ln1_g[0].reshape(1, d), ln1_b[0].reshape(1, d), layer=0, nblk=nblk,
                  x_specs=_token_specs(nbpb), row_fn=lambda j: _mod_row_all(j, nbpb), name="even_out_ln",
                  nbpb=nbpb)
    xa1 = _ffn_ln(x1, modt, ev_w_gate[0].astype(BF16), ev_w_up[0].astype(BF16), ev_w_down[0].astype(BF16),
                  ln2_g[0].reshape(1, d), ln2_b[0].reshape(1, d), nbpb=nbpb, nblk=nblk)

    cos32, sin32, partner32 = _rope_axis_tables(C_ROPE, seq, ctx_len)
    w_ext1, wq_pad, wq_padp, wk_pad, wv, pl2 = _odd_weights(od_w_in[0], od_w_q_up[0], od_w_kv_up[0], partner32)
    ntr = cos32.shape[0]
    pad_tail = jnp.zeros((ntr, LANES - C_NOPE - C_ROPE), F32)
    cos_q = jnp.concatenate([jnp.ones((ntr, C_NOPE), F32), cos32, pad_tail], axis=1)
    sin_q = jnp.concatenate([jnp.zeros((ntr, C_NOPE), F32), sin32, pad_tail], axis=1)
    cs_k = jnp.concatenate([cos32, sin32, jnp.zeros((ntr, LANES - 2 * C_ROPE), F32)], axis=1)
    qt1, k1, vt1, nq, nk, nv = _odd_inproj(
        xa1, modt, w_ext1, od_q_lat_gain[0].reshape(1, C_Q_RANK), od_kv_lat_gain[0].reshape(1, C_KV_RANK),
        wq_pad, wq_padp, wk_pad, wv, pl2, cos_q, sin_q, cs_k, batch=batch, nt=nt)
    mla = _flash(qt1, k1.reshape(batch, nt, C_HEADS * LANES), vt1, batch=batch, nt=nt, units=C_HEADS // 2,
                 nq=2, kw=2 * LANES, k_off=(0, LANES), v_off=(0, VROWS), ctx_queries=False, name="mla_flash")
    na = _na(nq, nk, nv, _na_bias_table(od_na_bias[0]), batch=batch, nt=nt, seq=seq)
    w_out1 = od_w_out[0].astype(BF16)
    x1b = _proj_ln(mla.reshape(batch * seq, C_HEADS * C_V), na.reshape(batch * seq, NA_W),
                   w_out1[:C_HEADS * C_V], w_out1[C_HEADS * C_V:], (xa1,), modt,
                   ln1_g[1].reshape(1, d), ln1_b[1].reshape(1, d), layer=1, nblk=nxb,
                   x_specs=[pl.BlockSpec((TM, D_MODEL), lambda j: (j + j // spb + 1, 0))],
                   row_fn=lambda j: 1 + j // spb, name="odd_out_ln")

    wr = jnp.zeros((d, LANES), F32).at[:, :N_EXPERTS].set(od_w_router[0])
    wr_hi = wr.astype(BF16)
    wr_lo = (wr - wr_hi.astype(F32)).astype(BF16)
    br = jnp.full((1, LANES), NEG, F32).at[0, :N_EXPERTS].set(od_b_router[0])
    tri = jnp.asarray(np.tril(np.ones((TM, TM), np.float32), -1), BF16)
    rt, cnt, u_moe = _router(x1b, modt, wr_hi, wr_lo, br, tri, nblk=nxb, spb=spb)

    t_tok = batch * seq
    counts = cnt[0, :N_EXPERTS].astype(jnp.int32)
    padded = (counts + MOE_BM - 1) // MOE_BM * MOE_BM
    pad_end = jnp.cumsum(padded)
    pad_start = pad_end - padded
    n_blocks = -(-2 * t_tok // MOE_BM) + N_EXPERTS
    e12 = rt[:, 0:2].astype(jnp.int32)
    r12 = rt[:, 4:6].astype(jnp.int32)
    dest = pad_start[e12] + r12
    blk_start = jnp.arange(n_blocks + 1, dtype=jnp.int32) * MOE_BM
    blk_expert = jnp.minimum(jnp.sum((pad_end[None, :] <= blk_start[:, None]).astype(jnp.int32), axis=1),
                             N_EXPERTS - 1)
    src_ext, dst_ext = _moe_tables(dest, t_tok, n_blocks)

    y2 = _experts(blk_expert, src_ext, dst_ext, u_moe, od_w_gate[0].astype(BF16), od_w_up[0].astype(BF16),
                  od_w_down[0].astype(BF16), n_out=2 * t_tok + 2 * MOE_BM)
    out = _combine(y2, x1b, rt, modt, ln2_g[1].reshape(1, d), ln2_b[1].reshape(1, d), nblk=nxb, spb=spb)
    return out.reshape(batch, seq, d)
```

```python
import functools
import math

import numpy as np
import jax
import jax.numpy as jnp
from jax import lax
from jax.experimental import pallas as pl
from jax.experimental.pallas import tpu as pltpu

F32 = jnp.float32
BF16 = jnp.bfloat16

D_MODEL = 1024
DEPTH = 2
GRID_W = 64
ROPE_THETA = 10000.0
NORM_EPS = 1e-6
ALPHA = (2 * DEPTH) ** 0.25
HEAD_DIM = 64
A_Q_HEADS = 12
A_KV_HEADS = 4
A_GROUP = A_Q_HEADS // A_KV_HEADS
POOL_WINDOWS = (2, 4, 8, 16)
POOL_GROUP = 64
POOL_WIDTH = POOL_GROUP * len(POOL_WINDOWS)
C_HEADS = 8
C_Q_RANK = 384
C_KV_RANK = 256
C_NOPE = 64
C_ROPE = 32
C_V = 64
D_HEADS = 8
NA_WIN_H = 8
NA_WIN_W = 16
D_FF = 2816
N_EXPERTS = 8
D_FF_EXPERT = 3584
A_Q_W = A_Q_HEADS * HEAD_DIM
A_KV_W = A_KV_HEADS * HEAD_DIM
NA_W = D_HEADS * HEAD_DIM

LANES = 128
TM = 256
VMEM_CAP = 56 * 1024 * 1024
MOE_BM = 512
MOE_TF = 1792
MOE_NF = D_FF_EXPERT // MOE_TF
VROWS = HEAD_DIM + 16
FLASH_STAGES = 130
FLASH_LOOKAHEAD = 6
NEG = -1e30
LOG2E = 1.4426950408889634


def _cparams(sem, vmem_bytes):
    return pltpu.CompilerParams(dimension_semantics=sem,
                                vmem_limit_bytes=int(min(VMEM_CAP, max(vmem_bytes, 16 * 1024 * 1024))))


def _dot(a, b):
    return jnp.dot(a, b, preferred_element_type=F32)


def _dot_nt(a, b):
    return lax.dot_general(a, b, (((1,), (1,)), ((), ())), preferred_element_type=F32)


def _layer_norm(z, g, b):
    mu = jnp.mean(z, axis=-1, keepdims=True)
    zc = z - mu
    var = jnp.mean(zc * zc, axis=-1, keepdims=True)
    return zc * lax.rsqrt(var + NORM_EPS) * g + b


def _ada_kernel(c_ref, w_ref, b_ref, o_ref):
    c = c_ref[...]
    s = c * jax.nn.sigmoid(c)
    o_ref[...] = _dot(s.astype(BF16), w_ref[...].astype(BF16)) + b_ref[...]


def _ada_table(cin, w_ada, b_ada):
    depth, d, n = w_ada.shape
    return pl.pallas_call(
        _ada_kernel,
        grid=(depth, n // d),
        in_specs=[pl.BlockSpec((8, d), lambda l, j: (0, 0)),
                  pl.BlockSpec((None, d, d), lambda l, j: (l, 0, j)),
                  pl.BlockSpec((None, 1, d), lambda l, j: (l, 0, j))],
        out_specs=pl.BlockSpec((None, 8, d), lambda l, j: (l, 0, j)),
        out_shape=jax.ShapeDtypeStruct((depth, 8, n), F32),
        compiler_params=_cparams(("arbitrary", "arbitrary"), 24 << 20),
        name="ada_table",
    )(cin, w_ada, b_ada.reshape(depth, 1, n))


def _mod_spec(layer, k, nargs):
    if nargs == 1:
        return pl.BlockSpec((None, 8, D_MODEL), lambda j: (layer, 0, k))
    return pl.BlockSpec((None, 8, D_MODEL), lambda *_: (layer, 0, k))


def _mod_row_all(j, nbpb):
    return jnp.where(j % nbpb == 0, 0, 1 + j // nbpb)


def _rope_axis_tables(rope_dims, seq, ctx):
    n = rope_dims // 2
    half = n // 2
    t = jnp.arange(seq, dtype=jnp.int32)
    row = (t // GRID_W).astype(F32)
    col = (t % GRID_W).astype(F32)
    inv = ROPE_THETA ** (-jnp.arange(half, dtype=F32) / half)
    cols_c, cols_s = [], []
    for pos in (row, col):
        ang = pos[:, None] * inv[None, :]
        c, s = jnp.cos(ang), jnp.sin(ang)
        cols_c += [c, c]
        cols_s += [-s, s]
    cos = jnp.concatenate(cols_c, axis=-1)
    sin = jnp.concatenate(cols_s, axis=-1)
    cos = jnp.concatenate([jnp.ones((ctx, rope_dims), F32), cos], axis=0)
    sin = jnp.concatenate([jnp.zeros((ctx, rope_dims), F32), sin], axis=0)
    d = np.arange(rope_dims)
    partner = (d // n) * n + ((d % n) + half) % n
    return cos, sin, partner


def _store_vt(vt_ref, vt, heads):
    ones = jnp.ones((VROWS - HEAD_DIM, TM), BF16)
    for h in range(heads):
        vt_ref[h * VROWS:h * VROWS + HEAD_DIM, :] = vt[h * HEAD_DIM:(h + 1) * HEAD_DIM, :]
        vt_ref[h * VROWS + HEAD_DIM:(h + 1) * VROWS, :] = ones


def _even_inproj_kernel(c_ref, x_ref, sh_ref, sc_ref, w_ref, cos_ref, sin_ref, gv_ref, ones_ref,
                        qt_ref, k_ref, vt_ref, p_ref, *, nbpb):
    j = pl.program_id(0)
    r = _mod_row_all(j, nbpb)
    shift = sh_ref[pl.ds(r, 1), :]
    scale = sc_ref[pl.ds(r, 1), :]
    u = _token_block(c_ref, x_ref, nbpb) * (1.0 + scale) + shift
    acc = _dot(u.astype(BF16), w_ref[...])
    cos = cos_ref[...]
    sin = sin_ref[...]
    ones_bd = ones_ref[...]

    def norm_rope(a, ap, g, gp):
        ssq = _dot((a * a).astype(BF16), ones_bd)
        rinv = lax.rsqrt(ssq * (1.0 / HEAD_DIM) + NORM_EPS)
        return (a * g * cos + ap * gp * sin) * rinv

    gq, gqp, gk, gkp = gv_ref[0:1, :], gv_ref[1:2, :], gv_ref[2:3, :], gv_ref[3:4, :]
    p_off = A_Q_W + 2 * A_KV_W + POOL_WIDTH
    zeros_half = jnp.zeros((HEAD_DIM, TM), BF16)
    for c in range(A_Q_W // LANES):
        a = acc[:, c * LANES:(c + 1) * LANES]
        ap = acc[:, p_off + c * LANES:p_off + (c + 1) * LANES]
        rot = norm_rope(a, ap, gq, gqp) * (HEAD_DIM ** -0.5 * LOG2E)
        rot_t = rot.T
        for hl in range(2):
            head = 2 * c + hl
            slot = (head // A_GROUP) % 2
            base = head * LANES
            qt_ref[base + slot * HEAD_DIM:base + (slot + 1) * HEAD_DIM, :] = (
                rot_t[hl * HEAD_DIM:(hl + 1) * HEAD_DIM, :].astype(BF16))
            qt_ref[base + (1 - slot) * HEAD_DIM:base + (2 - slot) * HEAD_DIM, :] = zeros_half
    for c in range(A_KV_W // LANES):
        a = acc[:, A_Q_W + c * LANES:A_Q_W + (c + 1) * LANES]
        ap = acc[:, p_off + A_Q_W + c * LANES:p_off + A_Q_W + (c + 1) * LANES]
        k_ref[:, c * LANES:(c + 1) * LANES] = norm_rope(a, ap, gk, gkp).astype(BF16)
    v = acc[:, A_Q_W + A_KV_W:A_Q_W + 2 * A_KV_W]
    _store_vt(vt_ref, v.T.astype(BF16), A_KV_HEADS)
    p_ref[...] = acc[:, A_Q_W + 2 * A_KV_W:A_Q_W + 2 * A_KV_W + POOL_WIDTH]


def _even_inproj(ctx2, x2, modt, w_ext, cos2, sin2, gvec, ones_bd, *, batch, nt):
    nbpb = nt // TM
    nblk = batch * nbpb
    n_ext = w_ext.shape[1]
    kern = functools.partial(_even_inproj_kernel, nbpb=nbpb)
    return pl.pallas_call(
        kern,
        grid=(nblk,),
        in_specs=[*_token_specs(nbpb),
                  _mod_spec(0, 0, 1), _mod_spec(0, 1, 1),
                  pl.BlockSpec((D_MODEL, n_ext), lambda j: (0, 0)),
                  pl.BlockSpec((TM, LANES), lambda j: (j % nbpb, 0)),
                  pl.BlockSpec((TM, LANES), lambda j: (j % nbpb, 0)),
                  pl.BlockSpec((8, LANES), lambda j: (0, 0)),
                  pl.BlockSpec((LANES, LANES), lambda j: (0, 0))],
        out_specs=[pl.BlockSpec((None, A_Q_HEADS * LANES, TM), lambda j: (j // nbpb, 0, j % nbpb)),
                   pl.BlockSpec((TM, A_KV_W), lambda j: (j, 0)),
                   pl.BlockSpec((None, None, A_KV_HEADS * VROWS, TM), lambda j: (j // nbpb, j % nbpb, 0, 0)),
                   pl.BlockSpec((TM, POOL_WIDTH), lambda j: (j, 0))],
        out_shape=[jax.ShapeDtypeStruct((batch, A_Q_HEADS * LANES, nt), BF16),
                   jax.ShapeDtypeStruct((batch * nt, A_KV_W), BF16),
                   jax.ShapeDtypeStruct((batch, nbpb, A_KV_HEADS * VROWS, TM), BF16),
                   jax.ShapeDtypeStruct((batch * nt, POOL_WIDTH), F32)],
        compiler_params=_cparams(("arbitrary",), 40 << 20),
        name="even_inproj",
    )(ctx2, x2, modt, modt, w_ext, cos2, sin2, gvec, ones_bd)


def _flash_kernel(q_ref, k_ref, v_ref, o_ref, acc_ref, m_ref, out_ref, *, nq, k_off, v_off, ctx_queries, nsub):
    nch = v_ref.shape[0]
    acc_ref[...] = jnp.zeros_like(acc_ref)
    m_ref[...] = jnp.full_like(m_ref, NEG)

    def scores(j, c):
        start = c * TM if isinstance(c, int) else pl.multiple_of(c * TM, TM)
        kc = k_ref[pl.ds(start, TM), k_off[j]:k_off[j] + LANES]
        return _dot(kc, q_ref[j * LANES:(j + 1) * LANES, :])

    def update(j, c, s):
        row = slice(j, j + 1)
        m_old = m_ref[row, :]
        m_new = jnp.maximum(m_old, jnp.max(s, axis=0, keepdims=True))
        alpha = jnp.exp2(m_old - m_new)
        p = jnp.exp2(s - m_new)
        m_ref[row, :] = m_new
        pv = _dot(v_ref[c, v_off[j]:v_off[j] + VROWS, :], p.astype(BF16))
        rows = slice(j * VROWS, (j + 1) * VROWS)
        acc_ref[rows, :] = alpha * acc_ref[rows, :] + pv

    def run(chunks):
        stages = [(j, c) for c in chunks for j in range(nq)]
        ahead = [scores(*st) for st in stages[:FLASH_LOOKAHEAD]]
        for i, (j, c) in enumerate(stages):
            if i + FLASH_LOOKAHEAD < len(stages):
                ahead.append(scores(*stages[i + FLASH_LOOKAHEAD]))
            update(j, c, ahead.pop(0))

    def all_chunks():
        if nsub == nch:
            run(list(range(nch)))
        else:
            def body(it, carry):
                run([it * nsub + t for t in range(nsub)])
                return carry

            lax.fori_loop(0, nch // nsub, body, 0)

    if ctx_queries:
        pl.when(pl.program_id(2) == 0)(lambda: run([0]))
        pl.when(pl.program_id(2) != 0)(all_chunks)
    else:
        all_chunks()
    for j in range(nq):
        denom = acc_ref[j * VROWS + HEAD_DIM:j * VROWS + HEAD_DIM + 1, :]
        out_ref[j * HEAD_DIM:(j + 1) * HEAD_DIM, :] = acc_ref[j * VROWS:j * VROWS + HEAD_DIM, :] / denom
    o_ref[...] = out_ref[...].T.astype(BF16)


def _flash(qt, k, vt, *, batch, nt, units, nq, kw, k_off, v_off, ctx_queries, name):
    nbpb = nt // TM
    nqb = nbpb if ctx_queries else nbpb - 1
    qoff = 0 if ctx_queries else 1
    nv = vt.shape[2] // units
    nsub = max(d for d in range(1, FLASH_STAGES // nq + 1) if nbpb % d == 0)
    kern = functools.partial(_flash_kernel, nq=nq, k_off=k_off, v_off=v_off, ctx_queries=ctx_queries, nsub=nsub)
    vmem = 2 * (nt * kw * 2 + nt * nv * 2) + 4 * nq * LANES * TM * 2 + 3 * nq * HEAD_DIM * TM * 4 + (8 << 20)
    return pl.pallas_call(
        kern,
        grid=(batch, units, nqb),
        in_specs=[pl.BlockSpec((None, nq * LANES, TM), lambda b, u, i: (b, u, i + qoff)),
                  pl.BlockSpec((None, nt, kw), lambda b, u, i: (b, 0, u)),
                  pl.BlockSpec((None, nbpb, nv, TM), lambda b, u, i: (b, 0, u, 0))],
        out_specs=pl.BlockSpec((None, TM, nq * HEAD_DIM), lambda b, u, i: (b, i, u)),
        out_shape=jax.ShapeDtypeStruct((batch, nqb * TM, units * nq * HEAD_DIM), BF16),
        scratch_shapes=[pltpu.VMEM((nq * VROWS, TM), F32),
                        pltpu.VMEM((8, TM), F32),
                        pltpu.VMEM((nq * HEAD_DIM, TM), F32)],
        compiler_params=_cparams(("arbitrary", "arbitrary", "arbitrary"), vmem),
        name=name,
    )(qt, k, vt)


def _pool_kernel(prev_ref, cur_ref, next_ref, w_ref, ps_ref, o_ref, *, nbpb, nt):
    jj = pl.program_id(0) % nbpb
    base = jj * TM
    seg_lo = jnp.where(jj == 0, 0, TM)
    seg_hi = jnp.where(jj == 0, TM, nt)
    cur = cur_ref[...]
    xcat = jnp.concatenate([prev_ref[...], cur, next_ref[...]], axis=0)
    x_hi = xcat.astype(BF16)
    x_lo = (xcat - x_hi.astype(F32)).astype(BF16)
    t = base + lax.broadcasted_iota(jnp.int32, (TM, 3 * TM), 0)
    s = base - TM + lax.broadcasted_iota(jnp.int32, (TM, 3 * TM), 1)
    t1 = base + lax.broadcasted_iota(jnp.int32, (TM, 1), 0)
    lane_grp = lax.broadcasted_iota(jnp.int32, (TM, POOL_WIDTH), 1) // POOL_GROUP
    mean = jnp.zeros((TM, POOL_WIDTH), F32)
    for g, win in enumerate(POOL_WINDOWS):
        lo = win // 2
        hi = win - lo
        valid = ((s >= jnp.maximum(t - lo, seg_lo)) & (s < jnp.minimum(t + hi, seg_hi)))
        vb = valid.astype(F32).astype(BF16)
        wsum = _dot(vb, x_hi) + _dot(vb, x_lo)
        cnt = (jnp.minimum(t1 + hi, seg_hi) - jnp.maximum(t1 - lo, seg_lo)).astype(F32)
        mean = jnp.where(lane_grp == g, wsum / cnt, mean)
    y = (mean - cur).astype(BF16)
    o_ref[...] = (_dot(y, w_ref[...]) * ps_ref[...]).astype(BF16)


def _pool(p, w_bd, ps, *, batch, nt):
    nbpb = nt // TM
    nblk = batch * nbpb
    kern = functools.partial(_pool_kernel, nbpb=nbpb, nt=nt)

    def prev_map(j):
        return (jnp.maximum(j - 1, 0), 0)

    def next_map(j):
        return (jnp.minimum(j + 1, nblk - 1), 0)

    return pl.pallas_call(
        kern,
        grid=(nblk,),
        in_specs=[pl.BlockSpec((TM, POOL_WIDTH), prev_map),
                  pl.BlockSpec((TM, POOL_WIDTH), lambda j: (j, 0)),
                  pl.BlockSpec((TM, POOL_WIDTH), next_map),
                  pl.BlockSpec((POOL_WIDTH, POOL_WIDTH), lambda j: (0, 0)),
                  pl.BlockSpec((1, POOL_WIDTH), lambda j: (0, 0))],
        out_specs=pl.BlockSpec((TM, POOL_WIDTH), lambda j: (j, 0)),
        out_shape=jax.ShapeDtypeStruct((batch * nt, POOL_WIDTH), BF16),
        compiler_params=_cparams(("arbitrary",), 24 << 20),
        name="pool_mixer",
    )(p, p, p, w_bd, ps)


def _token_specs(nbpb):
    spb = nbpb - 1
    return [pl.BlockSpec((TM, D_MODEL), lambda j: (j // nbpb, 0)),
            pl.BlockSpec((TM, D_MODEL), lambda j: ((j // nbpb) * spb + jnp.maximum(j % nbpb - 1, 0), 0))]


def _token_block(c_ref, x_ref, nbpb):
    return jnp.where(pl.program_id(0) % nbpb == 0, c_ref[...], x_ref[...])


def _proj_ln_kernel(a_ref, b_ref, wa_ref, wb_ref, *rest, row_fn, nbpb):
    if nbpb:
        c_ref, x_ref, gate_ref, g_ref, beta_ref, o_ref = rest
        x = _token_block(c_ref, x_ref, nbpb)
    else:
        x_ref, gate_ref, g_ref, beta_ref, o_ref = rest
        x = x_ref[...]
    r = row_fn(pl.program_id(0))
    gate = gate_ref[pl.ds(r, 1), :]
    y = _dot(a_ref[...], wa_ref[...]) + _dot(b_ref[...], wb_ref[...])
    o_ref[...] = _layer_norm(ALPHA * x + gate * y, g_ref[...], beta_ref[...])


def _proj_ln(a, b, wa, wb, xs, modt, g, beta, *, layer, nblk, x_specs, row_fn, name, nbpb=0):
    ka, kb = a.shape[1], b.shape[1]
    kern = functools.partial(_proj_ln_kernel, row_fn=row_fn, nbpb=nbpb)
    return pl.pallas_call(
        kern,
        grid=(nblk,),
        in_specs=[pl.BlockSpec((TM, ka), lambda j: (j, 0)),
                  pl.BlockSpec((TM, kb), lambda j: (j, 0)),
                  pl.BlockSpec((ka, D_MODEL), lambda j: (0, 0)),
                  pl.BlockSpec((kb, D_MODEL), lambda j: (0, 0)),
                  *x_specs,
                  _mod_spec(layer, 2, 1),
                  pl.BlockSpec((1, D_MODEL), lambda j: (0, 0)),
                  pl.BlockSpec((1, D_MODEL), lambda j: (0, 0))],
        out_specs=pl.BlockSpec((TM, D_MODEL), lambda j: (j, 0)),
        out_shape=jax.ShapeDtypeStruct((nblk * TM, D_MODEL), F32),
        compiler_params=_cparams(("arbitrary",), 32 << 20),
        name=name,
    )(a, b, wa, wb, *xs, modt, g, beta)


def _ffn_ln_kernel(x_ref, sh_ref, sc_ref, gate_ref, wg_ref, wu_ref, wd_ref, g_ref, beta_ref, o_ref, *, nbpb):
    r = _mod_row_all(pl.program_id(0), nbpb)
    x = x_ref[...]
    u = (x * (1.0 + sc_ref[pl.ds(r, 1), :]) + sh_ref[pl.ds(r, 1), :]).astype(BF16)
    hg = _dot(u, wg_ref[...])
    hu = _dot(u, wu_ref[...])
    h = (hg * jax.nn.sigmoid(hg) * hu).astype(BF16)
    y = _dot(h, wd_ref[...])
    gate = gate_ref[pl.ds(r, 1), :]
    o_ref[...] = _layer_norm(ALPHA * x + gate * y, g_ref[...], beta_ref[...])


def _resident(shape):
    return pl.BlockSpec(shape, lambda *_: (0,) * len(shape), pipeline_mode=pl.Buffered(1))


def _ffn_ln(x1, modt, wg, wu, wd, g, beta, *, nbpb, nblk):
    dff = wg.shape[1]
    kern = functools.partial(_ffn_ln_kernel, nbpb=nbpb)
    return pl.pallas_call(
        kern,
        grid=(nblk,),
        in_specs=[pl.BlockSpec((TM, D_MODEL), lambda j: (j, 0)),
                  _mod_spec(0, 3, 1), _mod_spec(0, 4, 1), _mod_spec(0, 5, 1),
                  _resident((D_MODEL, dff)), _resident((D_MODEL, dff)), _resident((dff, D_MODEL)),
                  pl.BlockSpec((1, D_MODEL), lambda j: (0, 0)),
                  pl.BlockSpec((1, D_MODEL), lambda j: (0, 0))],
        out_specs=pl.BlockSpec((TM, D_MODEL), lambda j: (j, 0)),
        out_shape=jax.ShapeDtypeStruct((nblk * TM, D_MODEL), F32),
        compiler_params=_cparams(("arbitrary",), 44 << 20),
        name="even_ffn_ln",
    )(x1, modt, modt, modt, wg, wu, wd, g, beta)


def _odd_inproj_kernel(x_ref, sh_ref, sc_ref, w_ref, gq_ref, gkv_ref, wq_ref, wqp_ref, wk_ref, wv_ref,
                       pl2_ref, cq_ref, sq_ref, cs_ref,
                       qt_ref, k_ref, vt_ref, nq_ref, nk_ref, nv_ref, *, nbpb):
    j = pl.program_id(0)
    r = _mod_row_all(j, nbpb)
    u = x_ref[...] * (1.0 + sc_ref[pl.ds(r, 1), :]) + sh_ref[pl.ds(r, 1), :]
    acc = _dot(u.astype(BF16), w_ref[...])
    o = 0
    cq = acc[:, o:o + C_Q_RANK]; o += C_Q_RANK
    ckv = acc[:, o:o + C_KV_RANK]; o += C_KV_RANK
    nq = acc[:, o:o + NA_W]; o += NA_W
    nk = acc[:, o:o + NA_W]; o += NA_W
    nv = acc[:, o:o + NA_W]; o += NA_W
    krc = acc[:, o:o + LANES]

    def rms(a, g):
        return (a * lax.rsqrt(jnp.mean(a * a, axis=-1, keepdims=True) + NORM_EPS) * g).astype(BF16)

    cqn = rms(cq, gq_ref[...])
    kvn = rms(ckv, gkv_ref[...])
    cos_q = jnp.tile(cq_ref[...], (1, C_HEADS))
    sin_q = jnp.tile(sq_ref[...], (1, C_HEADS))
    q = _dot(cqn, wq_ref[...]) * cos_q + _dot(cqn, wqp_ref[...]) * sin_q
    q = q * ((C_NOPE + C_ROPE) ** -0.5 * LOG2E)
    qt_ref[...] = q.T.astype(BF16)
    kr_terms = (krc * cs_ref[...]).astype(BF16)
    k_ref[...] = (_dot(kvn, wk_ref[...]) + _dot(kr_terms, pl2_ref[...])).astype(BF16)
    _store_vt(vt_ref, _dot(kvn, wv_ref[...]).T.astype(BF16), C_HEADS)
    nq_ref[...] = (nq * HEAD_DIM ** -0.5).astype(BF16)
    nk_ref[...] = nk.astype(BF16)
    nv_ref[...] = nv.astype(BF16)


def _odd_inproj(xa, modt, w_ext, gq, gkv, wq_pad, wq_padp, wk_pad, wv, pl2, cos_q, sin_q, cs_k, *, batch, nt):
    nbpb = nt // TM
    nblk = batch * nbpb
    n_ext = w_ext.shape[1]
    hq = C_HEADS * LANES
    hv = C_HEADS * C_V
    kern = functools.partial(_odd_inproj_kernel, nbpb=nbpb)
    const = lambda j: (0, 0)
    tok = lambda j: (j, 0)
    pos = lambda j: (j % nbpb, 0)
    return pl.pallas_call(
        kern,
        grid=(nblk,),
        in_specs=[pl.BlockSpec((TM, D_MODEL), tok),
                  _mod_spec(1, 0, 1), _mod_spec(1, 1, 1),
                  pl.BlockSpec((D_MODEL, n_ext), const),
                  pl.BlockSpec((1, C_Q_RANK), const),
                  pl.BlockSpec((1, C_KV_RANK), const),
                  pl.BlockSpec((C_Q_RANK, hq), const),
                  pl.BlockSpec((C_Q_RANK, hq), const),
                  pl.BlockSpec((C_KV_RANK, hq), const),
                  pl.BlockSpec((C_KV_RANK, hv), const),
                  pl.BlockSpec((LANES, hq), const),
                  pl.BlockSpec((TM, LANES), pos),
                  pl.BlockSpec((TM, LANES), pos),
                  pl.BlockSpec((TM, LANES), pos)],
        out_specs=[pl.BlockSpec((None, hq, TM), lambda j: (j // nbpb, 0, j % nbpb)),
                   pl.BlockSpec((TM, hq), tok),
                   pl.BlockSpec((None, None, C_HEADS * VROWS, TM), lambda j: (j // nbpb, j % nbpb, 0, 0)),
                   pl.BlockSpec((TM, NA_W), tok),
                   pl.BlockSpec((TM, NA_W), tok),
                   pl.BlockSpec((TM, NA_W), tok)],
        out_shape=[jax.ShapeDtypeStruct((batch, hq, nt), BF16),
                   jax.ShapeDtypeStruct((batch * nt, hq), BF16),
                   jax.ShapeDtypeStruct((batch, nbpb, C_HEADS * VROWS, TM), BF16),
                   jax.ShapeDtypeStruct((batch * nt, NA_W), BF16),
                   jax.ShapeDtypeStruct((batch * nt, NA_W), BF16),
                   jax.ShapeDtypeStruct((batch * nt, NA_W), BF16)],
        compiler_params=_cparams(("arbitrary",), 48 << 20),
        name="odd_inproj",
    )(xa, modt, modt, w_ext, gq, gkv, wq_pad, wq_padp, wk_pad, wv, pl2, cos_q, sin_q, cs_k)


def _na_kernel(q_ref, kb_ref, vb_ref, kc_ref, vc_ref, bias_ref, o_ref):
    lane = lax.broadcasted_iota(jnp.int32, (GRID_W, LANES), 1)
    first_half = (lane // HEAD_DIM) == 0
    scores = []
    for h in range(D_HEADS):
        cols = slice((h // 2) * LANES, (h // 2 + 1) * LANES)
        qp = q_ref[:, cols]
        qh = jnp.where(first_half == (h % 2 == 0), qp, jnp.zeros_like(qp))
        s_w = _dot_nt(qh, kb_ref[0, :, cols]) + bias_ref[h]
        s_c = _dot_nt(qh, kc_ref[:, cols])
        scores.append((s_w, s_c))
    outs = []
    for h in range(D_HEADS):
        cols = slice((h // 2) * LANES, (h // 2 + 1) * LANES)
        s_w, s_c = scores[h]
        m = jnp.maximum(jnp.max(s_w, axis=-1, keepdims=True), jnp.max(s_c, axis=-1, keepdims=True))
        p_w = jnp.exp(s_w - m)
        p_c = jnp.exp(s_c - m)
        l = jnp.sum(p_w, axis=-1, keepdims=True) + jnp.sum(p_c, axis=-1, keepdims=True)
        o = _dot(p_w.astype(BF16), vb_ref[0, :, cols]) + _dot(p_c.astype(BF16), vc_ref[:, cols])
        outs.append(o / l)
    for hp in range(D_HEADS // 2):
        o_ref[:, hp * LANES:(hp + 1) * LANES] = jnp.where(first_half, outs[2 * hp], outs[2 * hp + 1]).astype(BF16)


def _na(nq, nk, nv, bias_tab, *, batch, nt, seq):
    rows = seq // GRID_W
    kh = NA_WIN_H
    band = kh * GRID_W
    ctx_rows = TM // GRID_W

    def rs_of(r):
        return jnp.clip(r - kh // 2, 0, rows - kh)

    nq3 = nq.reshape(batch, nt, NA_W)
    nk3 = nk.reshape(batch, nt, NA_W)
    nv3 = nv.reshape(batch, nt, NA_W)
    band_spec = pl.BlockSpec((pl.Element(1), pl.Element(band), pl.Element(NA_W)),
                             lambda b, r: (b, pl.multiple_of(TM + rs_of(r) * GRID_W, GRID_W), 0))
    ctx_spec = pl.BlockSpec((None, TM, NA_W), lambda b, r: (b, 0, 0))
    return pl.pallas_call(
        _na_kernel,
        grid=(batch, rows),
        in_specs=[pl.BlockSpec((None, GRID_W, NA_W), lambda b, r: (b, ctx_rows + r, 0)),
                  band_spec, band_spec, ctx_spec, ctx_spec,
                  pl.BlockSpec((None, D_HEADS, GRID_W, band), lambda b, r: (rs_of(r) - r + NA_WIN_H - 1, 0, 0, 0))],
        out_specs=pl.BlockSpec((None, GRID_W, NA_W), lambda b, r: (b, r, 0)),
        out_shape=jax.ShapeDtypeStruct((batch, seq, NA_W), BF16),
        compiler_params=_cparams(("arbitrary", "arbitrary"), 24 << 20),
        name="neighborhood_attn",
    )(nq3, nk3, nv3, nk3, nv3, bias_tab)


def _na_bias_table(rel_bias):
    j = np.arange(GRID_W)
    col_start = np.clip(j - NA_WIN_W // 2, 0, GRID_W - NA_WIN_W)
    col_valid = (j[None, :] >= col_start[:, None]) & (j[None, :] < col_start[:, None] + NA_WIN_W)
    col_idx = np.clip(j[None, :] - j[:, None] + NA_WIN_W - 1, 0, 2 * NA_WIN_W - 2)
    ds = np.arange(NA_WIN_H)[:, None] + np.arange(NA_WIN_H)[None, :]
    onehot = jnp.asarray(col_idx[:, :, None] == np.arange(2 * NA_WIN_W - 1), F32)
    tab = jnp.einsum('hdic,qkc->hdiqk', rel_bias[:, ds], onehot,
                     precision=lax.Precision.HIGHEST)
    tab = jnp.where(col_valid[None, None, None], tab, NEG)
    tab = tab.transpose(1, 0, 3, 2, 4)
    return tab.reshape(NA_WIN_H, D_HEADS, GRID_W, NA_WIN_H * GRID_W).astype(F32)


def _router_kernel(x_ref, sh_ref, sc_ref, wh_ref, wl_ref, b_ref, tri_ref, o_ref, cnt_ref, u_ref, base_ref, *, spb):
    j = pl.program_id(0)

    @pl.when(j == 0)
    def _():
        base_ref[...] = jnp.zeros_like(base_ref)

    r = 1 + j // spb
    u = x_ref[...] * (1.0 + sc_ref[pl.ds(r, 1), :]) + sh_ref[pl.ds(r, 1), :]
    u_ref[...] = u
    u_hi = u.astype(BF16)
    u_lo = (u - u_hi.astype(F32)).astype(BF16)
    wh = wh_ref[...]
    logits = _dot(u_hi, wh) + _dot(u_lo, wh) + _dot(u_hi, wl_ref[...]) + b_ref[...]
    lane = lax.broadcasted_iota(jnp.int32, (TM, LANES), 1)
    m1 = jnp.max(logits, axis=-1, keepdims=True)
    i1 = jnp.min(jnp.where(logits == m1, lane, LANES), axis=-1, keepdims=True)
    rest = jnp.where(lane == i1, NEG * 2, logits)
    m2 = jnp.max(rest, axis=-1, keepdims=True)
    i2 = jnp.min(jnp.where(rest == m2, lane, LANES), axis=-1, keepdims=True)
    e21 = jnp.exp(m2 - m1)
    g1 = 1.0 / (1.0 + e21)
    g2 = e21 / (1.0 + e21)
    oh1 = (lane == i1).astype(F32)
    oh2 = (lane == i2).astype(F32)
    both = oh1 + oh2
    before = _dot(tri_ref[...], both.astype(BF16)) + base_ref[0:1, :]
    r1 = jnp.sum(oh1 * before, axis=-1, keepdims=True)
    r2 = jnp.sum(oh2 * before, axis=-1, keepdims=True)
    out = jnp.zeros((TM, LANES), F32)
    for idx, col in enumerate((i1.astype(F32), i2.astype(F32), g1, g2, r1, r2)):
        out = jnp.where(lane == idx, col, out)
    o_ref[...] = out
    total = base_ref[0:1, :] + jnp.sum(both, axis=0, keepdims=True)
    base_ref[...] = jnp.broadcast_to(total, base_ref.shape)
    cnt_ref[...] = jnp.broadcast_to(total, cnt_ref.shape)


def _router(x1, modt, wr_hi, wr_lo, br, tri, *, nblk, spb):
    kern = functools.partial(_router_kernel, spb=spb)
    return pl.pallas_call(
        kern,
        grid=(nblk,),
        in_specs=[pl.BlockSpec((TM, D_MODEL), lambda j: (j, 0)),
                  _mod_spec(1, 3, 1), _mod_spec(1, 4, 1),
                  pl.BlockSpec((D_MODEL, LANES), lambda j: (0, 0)),
                  pl.BlockSpec((D_MODEL, LANES), lambda j: (0, 0)),
                  pl.BlockSpec((1, LANES), lambda j: (0, 0)),
                  pl.BlockSpec((TM, TM), lambda j: (0, 0))],
        out_specs=[pl.BlockSpec((TM, LANES), lambda j: (j, 0)),
                   pl.BlockSpec((8, LANES), lambda j: (0, 0)),
                   pl.BlockSpec((TM, D_MODEL), lambda j: (j, 0))],
        out_shape=[jax.ShapeDtypeStruct((nblk * TM, LANES), F32),
                   jax.ShapeDtypeStruct((8, LANES), F32),
                   jax.ShapeDtypeStruct((nblk * TM, D_MODEL), F32)],
        scratch_shapes=[pltpu.VMEM((8, LANES), F32)],
        compiler_params=_cparams(("arbitrary",), 24 << 20),
        name="moe_router",
    )(x1, modt, modt, wr_hi, wr_lo, br, tri)


def _expert_kernel(be_ref, src0_ref, srcn_ref, dstp_ref, u_ref, wg_ref, wu_ref, wd_ref, y_ref,
                   xbuf_ref, xb_ref, obuf_ref, gsem, ssem):
    del be_ref
    i = pl.program_id(0)
    f = pl.program_id(1)
    last_i = pl.num_programs(0) - 1
    slot = i % 2
    other = 1 - slot
    half = MOE_BM // MOE_NF

    def gather_row(src_ref, r, dst_slot):
        return pltpu.make_async_copy(u_ref.at[pl.ds(src_ref[0, r], 1)],
                                     xbuf_ref.at[dst_slot, pl.ds(r, 1)], gsem.at[dst_slot])

    def scatter_row(r, src_slot):
        return pltpu.make_async_copy(obuf_ref.at[src_slot, pl.ds(r, 1)],
                                     y_ref.at[pl.ds(dstp_ref[0, r], 1)], ssem.at[src_slot])

    def gather_block_wait(s):
        pltpu.make_async_copy(u_ref.at[pl.ds(0, MOE_BM)], xbuf_ref.at[s], gsem.at[s]).wait()

    def scatter_block_wait(s):
        pltpu.make_async_copy(obuf_ref.at[s], y_ref.at[pl.ds(0, MOE_BM)], ssem.at[s]).wait()

    @pl.when((i == 0) & (f == 0))
    def _():
        obuf_ref[...] = jnp.zeros_like(obuf_ref)

        def first(r, c):
            gather_row(src0_ref, r, 0).start()
            return c

        lax.fori_loop(0, MOE_BM, first, 0)

    @pl.when(f == 0)
    def _():
        gather_block_wait(slot)
        xb_ref[...] = xbuf_ref[slot].astype(BF16)

    @pl.when((f == 0) & (i >= 1))
    def _():
        scatter_block_wait(slot)

    for t in range(half):
        r = f * half + t
        gather_row(srcn_ref, r, other).start(priority=0)
        scatter_row(r, other).start(priority=1)

    xb = xb_ref[...]
    hg = _dot(xb, wg_ref[...])
    hu = _dot(xb, wu_ref[...])
    h = (hg * jax.nn.sigmoid(hg) * hu).astype(BF16)
    part = _dot(h, wd_ref[...])

    @pl.when(f == 0)
    def _():
        obuf_ref[slot] = part

    @pl.when(f > 0)
    def _():
        obuf_ref[slot] += part

    @pl.when((i == last_i) & (f == MOE_NF - 1))
    def _():
        gather_block_wait(other)
        scatter_block_wait(other)


def _experts(blk_expert, src_ext, dst_ext, u, wg, wu, wd, *, n_out):
    nb = src_ext.shape[0] - 2
    assert wg.shape[2] == MOE_NF * MOE_TF and dst_ext.shape[0] == nb + 1
    idx = lambda fn: pl.BlockSpec((None, 1, MOE_BM), fn, memory_space=pltpu.SMEM)
    grid_spec = pltpu.PrefetchScalarGridSpec(
        num_scalar_prefetch=1,
        grid=(nb + 1, MOE_NF),
        in_specs=[idx(lambda i, f, be: (0, 0, 0)),
                  idx(lambda i, f, be: (i + 1, 0, 0)),
                  idx(lambda i, f, be: (i, 0, 0)),
                  pl.BlockSpec(memory_space=pl.ANY),
                  pl.BlockSpec((None, D_MODEL, MOE_TF), lambda i, f, be: (be[i], 0, f)),
                  pl.BlockSpec((None, D_MODEL, MOE_TF), lambda i, f, be: (be[i], 0, f)),
                  pl.BlockSpec((None, MOE_TF, D_MODEL), lambda i, f, be: (be[i], f, 0))],
        out_specs=pl.BlockSpec(memory_space=pl.ANY),
        scratch_shapes=[pltpu.VMEM((2, MOE_BM, D_MODEL), F32),
                        pltpu.VMEM((MOE_BM, D_MODEL), BF16),
                        pltpu.VMEM((2, MOE_BM, D_MODEL), F32),
                        pltpu.SemaphoreType.DMA((2,)),
                        pltpu.SemaphoreType.DMA((2,))],
    )
    return pl.pallas_call(
        _expert_kernel,
        grid_spec=grid_spec,
        out_shape=jax.ShapeDtypeStruct((n_out, D_MODEL), F32),
        compiler_params=_cparams(("arbitrary", "arbitrary"), 52 << 20),
        name="moe_experts",
    )(blk_expert, src_ext, src_ext, dst_ext, u, wg, wu, wd)


def _moe_tables(dest, t_tok, n_blocks):
    npad = n_blocks * MOE_BM
    pos = jnp.arange(npad, dtype=jnp.int32)
    trash = 2 * t_tok + ((pos // MOE_BM) % 2) * MOE_BM + pos % MOE_BM
    flat = dest.reshape(-1)
    asg = jnp.arange(2 * t_tok, dtype=jnp.int32)
    tok, slot = asg // 2, asg % 2
    init = jnp.stack([jnp.zeros((npad,), jnp.int32), trash], axis=1)
    both = init.at[flat].set(jnp.stack([tok, slot * t_tok + tok], axis=1),
                             unique_indices=True, mode='promise_in_bounds')
    src, dst = both[:, 0], both[:, 1]
    src_ext = jnp.concatenate([src, jnp.zeros((2 * MOE_BM,), jnp.int32)]).reshape(n_blocks + 2, 1, MOE_BM)
    lead = 2 * t_tok + MOE_BM + jnp.arange(MOE_BM, dtype=jnp.int32)
    dst_ext = jnp.concatenate([lead, dst]).reshape(n_blocks + 1, 1, MOE_BM)
    return src_ext, dst_ext


def _combine_kernel(y0_ref, y1_ref, x_ref, rt_ref, gate_ref, g_ref, beta_ref, o_ref, *, spb):
    r = 1 + pl.program_id(0) // spb
    rt = rt_ref[...]
    y = rt[:, 2:3] * y0_ref[...] + rt[:, 3:4] * y1_ref[...]
    gate = gate_ref[pl.ds(r, 1), :]
    o_ref[...] = _layer_norm(ALPHA * x_ref[...] + gate * y, g_ref[...], beta_ref[...])


def _combine(y2, x1, rt, modt, g, beta, *, nblk, spb):
    kern = functools.partial(_combine_kernel, spb=spb)
    return pl.pallas_call(
        kern,
        grid=(nblk,),
        in_specs=[pl.BlockSpec((TM, D_MODEL), lambda j: (j, 0)),
                  pl.BlockSpec((TM, D_MODEL), lambda j: (j + nblk, 0)),
                  pl.BlockSpec((TM, D_MODEL), lambda j: (j, 0)),
                  pl.BlockSpec((TM, LANES), lambda j: (j, 0)),
                  _mod_spec(1, 5, 1),
                  pl.BlockSpec((1, D_MODEL), lambda j: (0, 0)),
                  pl.BlockSpec((1, D_MODEL), lambda j: (0, 0))],
        out_specs=pl.BlockSpec((TM, D_MODEL), lambda j: (j, 0)),
        out_shape=jax.ShapeDtypeStruct((nblk * TM, D_MODEL), F32),
        compiler_params=_cparams(("arbitrary",), 24 << 20),
        name="moe_combine_ln",
    )(y2, y2, x1, rt, modt, g, beta)


def _head_perm(partner, heads, width):
    return np.concatenate([h * width + partner for h in range(heads)])


def _even_weights(w_in, q_gain, k_gain, partner):
    perm_q = _head_perm(partner, A_Q_HEADS, HEAD_DIM)
    perm_k = _head_perm(partner, A_KV_HEADS, HEAD_DIM)
    wq = w_in[:, :A_Q_W]
    wk = w_in[:, A_Q_W:A_Q_W + A_KV_W]
    w_ext = jnp.concatenate([w_in, wq[:, perm_q], wk[:, perm_k]], axis=1).astype(BF16)
    gvec = jnp.zeros((8, LANES), F32)
    gvec = gvec.at[0].set(jnp.tile(q_gain, 2)).at[1].set(jnp.tile(q_gain[partner], 2))
    gvec = gvec.at[2].set(jnp.tile(k_gain, 2)).at[3].set(jnp.tile(k_gain[partner], 2))
    return w_ext, gvec


def _block_diag(blocks):
    n = blocks.shape[0]
    w = blocks.shape[1]
    out = jnp.zeros((n * w, n * w), blocks.dtype)
    for g in range(n):
        out = out.at[g * w:(g + 1) * w, g * w:(g + 1) * w].set(blocks[g])
    return out


def _odd_weights(w_in, w_q_up, w_kv_up, partner):
    o_kr = C_Q_RANK + C_KV_RANK
    w_kr = w_in[:, o_kr:o_kr + C_ROPE]
    w_rest = jnp.concatenate([w_in[:, :o_kr], w_in[:, o_kr + C_ROPE:]], axis=1)
    kr_blk = jnp.concatenate([w_kr, w_kr[:, partner], jnp.zeros((D_MODEL, LANES - 2 * C_ROPE), F32)], axis=1)
    w_ext = jnp.concatenate([w_rest, kr_blk], axis=1).astype(BF16)
    dq = C_NOPE + C_ROPE
    wq3 = w_q_up.reshape(C_Q_RANK, C_HEADS, dq)
    zq = jnp.zeros((C_Q_RANK, C_HEADS, LANES - dq), F32)
    wq_pad = jnp.concatenate([wq3, zq], axis=-1).reshape(C_Q_RANK, C_HEADS * LANES)
    wq_rope_p = wq3[:, :, C_NOPE:][:, :, partner]
    wq_padp = jnp.concatenate([jnp.zeros((C_Q_RANK, C_HEADS, C_NOPE), F32), wq_rope_p, zq], axis=-1)
    wq_padp = wq_padp.reshape(C_Q_RANK, C_HEADS * LANES)
    wkv3 = w_kv_up.reshape(C_KV_RANK, C_HEADS, C_NOPE + C_V)
    wk_pad = jnp.concatenate([wkv3[:, :, :C_NOPE], jnp.zeros((C_KV_RANK, C_HEADS, LANES - C_NOPE), F32)], axis=-1)
    wk_pad = wk_pad.reshape(C_KV_RANK, C_HEADS * LANES)
    wv = wkv3[:, :, C_NOPE:].reshape(C_KV_RANK, C_HEADS * C_V)
    pl2 = np.zeros((LANES, C_HEADS * LANES), np.float32)
    for h in range(C_HEADS):
        for dd in range(C_ROPE):
            pl2[dd, h * LANES + C_NOPE + dd] = 1.0
            pl2[C_ROPE + dd, h * LANES + C_NOPE + dd] = 1.0
    return (w_ext, wq_pad.astype(BF16), wq_padp.astype(BF16), wk_pad.astype(BF16), wv.astype(BF16),
            jnp.asarray(pl2, BF16))


def kernel(x, c, ctx, c_ctx, w_ada, b_ada, ln1_g, ln1_b, ln2_g, ln2_b,
           ev_w_in, ev_w_out, ev_q_gain, ev_k_gain, ev_w_pool, ev_pool_scale,
           ev_w_gate, ev_w_up, ev_w_down,
           od_w_in, od_w_out, od_q_lat_gain, od_kv_lat_gain, od_w_q_up, od_w_kv_up, od_na_bias,
           od_w_router, od_b_router, od_w_gate, od_w_up, od_w_down):
    batch, seq, d = x.shape
    ctx_len = ctx.shape[1]
    assert d == D_MODEL and ctx_len == TM and seq % TM == 0 and seq % GRID_W == 0
    assert seq // GRID_W >= NA_WIN_H and batch + 1 <= 8
    nt = ctx_len + seq
    nbpb = nt // TM
    nblk = batch * nbpb
    spb = seq // TM
    nxb = batch * spb

    ctx2 = ctx.reshape(batch * ctx_len, d)
    x2 = x.reshape(batch * seq, d)
    cin = jnp.zeros((8, d), F32).at[0].set(c_ctx).at[1:1 + batch].set(c)
    modt = _ada_table(cin, w_ada, b_ada)

    cos64, sin64, partner64 = _rope_axis_tables(HEAD_DIM, seq, ctx_len)
    w_ext0, gvec = _even_weights(ev_w_in[0], ev_q_gain[0], ev_k_gain[0], partner64)
    cos2 = jnp.tile(cos64, (1, 2))
    sin2 = jnp.tile(sin64, (1, 2))
    ones_bd = _block_diag(jnp.ones((2, HEAD_DIM, HEAD_DIM), BF16))
    qt, kk, vt, pp = _even_inproj(ctx2, x2, modt, w_ext0, cos2, sin2, gvec, ones_bd, batch=batch, nt=nt)
    attn = _flash(qt, kk.reshape(batch, nt, A_KV_W), vt, batch=batch, nt=nt, units=A_KV_HEADS // 2,
                  nq=2 * A_GROUP, kw=LANES, k_off=(0,) * (2 * A_GROUP),
                  v_off=tuple((jq // A_GROUP) * VROWS for jq in range(2 * A_GROUP)),
                  ctx_queries=True, name="gqa_flash")
    pooled = _pool(pp, _block_diag(ev_w_pool[0]).astype(BF16), ev_pool_scale[0].reshape(1, POOL_WIDTH),
                   batch=batch, nt=nt)
    w_out0 = ev_w_out[0].astype(BF16)
    x1 = _proj_ln(attn.reshape(batch * nt, A_Q_W), pooled, w_out0[:A_Q_W], w_out0[A_Q_W:], (ctx2, x2), modt,
                  ln1_g[0].reshape(1, d), ln1_b[0].reshape(1, d), layer=0, nblk=nblk,
                  x_specs=_token_specs(nbpb), row_fn=lambda j: _mod_row_all(j, nbpb), name="even_out_ln",
                  nbpb=nbpb)
    xa1 = _ffn_ln(x1, modt, ev_w_gate[0].astype(BF16), ev_w_up[0].astype(BF16), ev_w_down[0].astype(BF16),
                  ln2_g[0].reshape(1, d), ln2_b[0].reshape(1, d), nbpb=nbpb, nblk=nblk)

    cos32, sin32, partner32 = _rope_axis_tables(C_ROPE, seq, ctx_len)
    w_ext1, wq_pad, wq_padp, wk_pad, wv, pl2 = _odd_weights(od_w_in[0], od_w_q_up[0], od_w_kv_up[0], partner32)
    ntr = cos32.shape[0]
    pad_tail = jnp.zeros((ntr, LANES - C_NOPE - C_ROPE), F32)
    cos_q = jnp.concatenate([jnp.ones((ntr, C_NOPE), F32), cos32, pad_tail], axis=1)
    sin_q = jnp.concatenate([jnp.zeros((ntr, C_NOPE), F32), sin32, pad_tail], axis=1)
    cs_k = jnp.concatenate([cos32, sin32, jnp.zeros((ntr, LANES - 2 * C_ROPE), F32)], axis=1)
    qt1, k1, vt1, nq, nk, nv = _odd_inproj(
        xa1, modt, w_ext1, od_q_lat_gain[0].reshape(1, C_Q_RANK), od_kv_lat_gain[0].reshape(1, C_KV_RANK),
        wq_pad, wq_padp, wk_pad, wv, pl2, cos_q, sin_q, cs_k, batch=batch, nt=nt)
    mla = _flash(qt1, k1.reshape(batch, nt, C_HEADS * LANES), vt1, batch=batch, nt=nt, units=C_HEADS // 2,
                 nq=2, kw=2 * LANES, k_off=(0, LANES), v_off=(0, VROWS), ctx_queries=False, name="mla_flash")
    na = _na(nq, nk, nv, _na_bias_table(od_na_bias[0]), batch=batch, nt=nt, seq=seq)
    w_out1 = od_w_out[0].astype(BF16)
    x1b = _proj_ln(mla.reshape(batch * seq, C_HEADS * C_V), na.reshape(batch * seq, NA_W),
                   w_out1[:C_HEADS * C_V], w_out1[C_HEADS * C_V:], (xa1,), modt,
                   ln1_g[1].reshape(1, d), ln1_b[1].reshape(1, d), layer=1, nblk=nxb,
                   x_specs=[pl.BlockSpec((TM, D_MODEL), lambda j: (j + j // spb + 1, 0))],
                   row_fn=lambda j: 1 + j // spb, name="odd_out_ln")

    wr = jnp.zeros((d, LANES), F32).at[:, :N_EXPERTS].set(od_w_router[0])
    wr_hi = wr.astype(BF16)
    wr_lo = (wr - wr_hi.astype(F32)).astype(BF16)
    br = jnp.full((1, LANES), NEG, F32).at[0, :N_EXPERTS].set(od_b_router[0])
    tri = jnp.asarray(np.tril(np.ones((TM, TM), np.float32), -1), BF16)
    rt, cnt, u_moe = _router(x1b, modt, wr_hi, wr_lo, br, tri, nblk=nxb, spb=spb)

    t_tok = batch * seq
    counts = cnt[0, :N_EXPERTS].astype(jnp.int32)
    padded = (counts + MOE_BM - 1) // MOE_BM * MOE_BM
    pad_end = jnp.cumsum(padded)
    pad_start = pad_end - padded
    n_blocks = -(-2 * t_tok // MOE_BM) + N_EXPERTS
    e12 = rt[:, 0:2].astype(jnp.int32)
    r12 = rt[:, 4:6].astype(jnp.int32)
    dest = pad_start[e12] + r12
    blk_start = jnp.arange(n_blocks + 1, dtype=jnp.int32) * MOE_BM
    blk_expert = jnp.minimum(jnp.sum((pad_end[None, :] <= blk_start[:, None]).astype(jnp.int32), axis=1),
                             N_EXPERTS - 1)
    src_ext, dst_ext = _moe_tables(dest, t_tok, n_blocks)

    y2 = _experts(blk_expert, src_ext, dst_ext, u_moe, od_w_gate[0].astype(BF16), od_w_up[0].astype(BF16),
                  od_w_down[0].astype(BF16), n_out=2 * t_tok + 2 * MOE_BM)
    out = _combine(y2, x1b, rt, modt, ln2_g[1].reshape(1, d), ln2_b[1].reshape(1, d), nblk=nxb, spb=spb)
    return out.reshape(batch, seq, d)
```

```python
import functools
import math

import numpy as np
import jax
import jax.numpy as jnp
from jax import lax
from jax.experimental import pallas as pl
from jax.experimental.pallas import tpu as pltpu

F32 = jnp.float32
BF16 = jnp.bfloat16

D_MODEL = 1024
DEPTH = 2
GRID_W = 64
ROPE_THETA = 10000.0
NORM_EPS = 1e-6
ALPHA = (2 * DEPTH) ** 0.25
HEAD_DIM = 64
A_Q_HEADS = 12
A_KV_HEADS = 4
A_GROUP = A_Q_HEADS // A_KV_HEADS
POOL_WINDOWS = (2, 4, 8, 16)
POOL_GROUP = 64
POOL_WIDTH = POOL_GROUP * len(POOL_WINDOWS)
C_HEADS = 8
C_Q_RANK = 384
C_KV_RANK = 256
C_NOPE = 64
C_ROPE = 32
C_V = 64
D_HEADS = 8
NA_WIN_H = 8
NA_WIN_W = 16
D_FF = 2816
N_EXPERTS = 8
D_FF_EXPERT = 3584
A_Q_W = A_Q_HEADS * HEAD_DIM
A_KV_W = A_KV_HEADS * HEAD_DIM
NA_W = D_HEADS * HEAD_DIM

LANES = 128
TM = 256
VMEM_CAP = 56 * 1024 * 1024
MOE_BM = 512
MOE_TF = 1792
MOE_NF = D_FF_EXPERT // MOE_TF
VROWS = HEAD_DIM + 16
FLASH_STAGES = 130
FLASH_LOOKAHEAD = 5
NEG = -1e30
LOG2E = 1.4426950408889634


def _cparams(sem, vmem_bytes):
    return pltpu.CompilerParams(dimension_semantics=sem,
                                vmem_limit_bytes=int(min(VMEM_CAP, max(vmem_bytes, 16 * 1024 * 1024))))


def _dot(a, b):
    return jnp.dot(a, b, preferred_element_type=F32)


def _dot_nt(a, b):
    return lax.dot_general(a, b, (((1,), (1,)), ((), ())), preferred_element_type=F32)


def _layer_norm(z, g, b):
    mu = jnp.mean(z, axis=-1, keepdims=True)
    zc = z - mu
    var = jnp.mean(zc * zc, axis=-1, keepdims=True)
    return zc * lax.rsqrt(var + NORM_EPS) * g + b


def _ada_kernel(c_ref, w_ref, b_ref, o_ref):
    c = c_ref[...]
    s = c * jax.nn.sigmoid(c)
    o_ref[...] = _dot(s.astype(BF16), w_ref[...].astype(BF16)) + b_ref[...]


def _ada_table(cin, w_ada, b_ada):
    depth, d, n = w_ada.shape
    return pl.pallas_call(
        _ada_kernel,
        grid=(depth, n // d),
        in_specs=[pl.BlockSpec((8, d), lambda l, j: (0, 0)),
                  pl.BlockSpec((None, d, d), lambda l, j: (l, 0, j)),
                  pl.BlockSpec((None, 1, d), lambda l, j: (l, 0, j))],
        out_specs=pl.BlockSpec((None, 8, d), lambda l, j: (l, 0, j)),
        out_shape=jax.ShapeDtypeStruct((depth, 8, n), F32),
        compiler_params=_cparams(("arbitrary", "arbitrary"), 24 << 20),
        name="ada_table",
    )(cin, w_ada, b_ada.reshape(depth, 1, n))


def _mod_spec(layer, k, nargs):
    if nargs == 1:
        return pl.BlockSpec((None, 8, D_MODEL), lambda j: (layer, 0, k))
    return pl.BlockSpec((None, 8, D_MODEL), lambda *_: (layer, 0, k))


def _mod_row_all(j, nbpb):
    return jnp.where(j % nbpb == 0, 0, 1 + j // nbpb)


def _rope_axis_tables(rope_dims, seq, ctx):
    n = rope_dims // 2
    half = n // 2
    t = jnp.arange(seq, dtype=jnp.int32)
    row = (t // GRID_W).astype(F32)
    col = (t % GRID_W).astype(F32)
    inv = ROPE_THETA ** (-jnp.arange(half, dtype=F32) / half)
    cols_c, cols_s = [], []
    for pos in (row, col):
        ang = pos[:, None] * inv[None, :]
        c, s = jnp.cos(ang), jnp.sin(ang)
        cols_c += [c, c]
        cols_s += [-s, s]
    cos = jnp.concatenate(cols_c, axis=-1)
    sin = jnp.concatenate(cols_s, axis=-1)
    cos = jnp.concatenate([jnp.ones((ctx, rope_dims), F32), cos], axis=0)
    sin = jnp.concatenate([jnp.zeros((ctx, rope_dims), F32), sin], axis=0)
    d = np.arange(rope_dims)
    partner = (d // n) * n + ((d % n) + half) % n
    return cos, sin, partner


def _store_vt(vt_ref, vt, heads):
    ones = jnp.ones((VROWS - HEAD_DIM, TM), BF16)
    for h in range(heads):
        vt_ref[h * VROWS:h * VROWS + HEAD_DIM, :] = vt[h * HEAD_DIM:(h + 1) * HEAD_DIM, :]
        vt_ref[h * VROWS + HEAD_DIM:(h + 1) * VROWS, :] = ones


def _even_inproj_kernel(c_ref, x_ref, sh_ref, sc_ref, w_ref, cos_ref, sin_ref, gv_ref, ones_ref,
                        qt_ref, k_ref, vt_ref, p_ref, *, nbpb):
    j = pl.program_id(0)
    r = _mod_row_all(j, nbpb)
    shift = sh_ref[pl.ds(r, 1), :]
    scale = sc_ref[pl.ds(r, 1), :]
    u = _token_block(c_ref, x_ref, nbpb) * (1.0 + scale) + shift
    acc = _dot(u.astype(BF16), w_ref[...])
    cos = cos_ref[...]
    sin = sin_ref[...]
    ones_bd = ones_ref[...]

    def norm_rope(a, ap, g, gp):
        ssq = _dot((a * a).astype(BF16), ones_bd)
        rinv = lax.rsqrt(ssq * (1.0 / HEAD_DIM) + NORM_EPS)
        return (a * g * cos + ap * gp * sin) * rinv

    gq, gqp, gk, gkp = gv_ref[0:1, :], gv_ref[1:2, :], gv_ref[2:3, :], gv_ref[3:4, :]
    p_off = A_Q_W + 2 * A_KV_W + POOL_WIDTH
    zeros_half = jnp.zeros((HEAD_DIM, TM), BF16)
    for c in range(A_Q_W // LANES):
        a = acc[:, c * LANES:(c + 1) * LANES]
        ap = acc[:, p_off + c * LANES:p_off + (c + 1) * LANES]
        rot = norm_rope(a, ap, gq, gqp) * (HEAD_DIM ** -0.5 * LOG2E)
        rot_t = rot.T
        for hl in range(2):
            head = 2 * c + hl
            slot = (head // A_GROUP) % 2
            base = head * LANES
            qt_ref[base + slot * HEAD_DIM:base + (slot + 1) * HEAD_DIM, :] = (
                rot_t[hl * HEAD_DIM:(hl + 1) * HEAD_DIM, :].astype(BF16))
            qt_ref[base + (1 - slot) * HEAD_DIM:base + (2 - slot) * HEAD_DIM, :] = zeros_half
    for c in range(A_KV_W // LANES):
        a = acc[:, A_Q_W + c * LANES:A_Q_W + (c + 1) * LANES]
        ap = acc[:, p_off + A_Q_W + c * LANES:p_off + A_Q_W + (c + 1) * LANES]
        k_ref[:, c * LANES:(c + 1) * LANES] = norm_rope(a, ap, gk, gkp).astype(BF16)
    v = acc[:, A_Q_W + A_KV_W:A_Q_W + 2 * A_KV_W]
    _store_vt(vt_ref, v.T.astype(BF16), A_KV_HEADS)
    p_ref[...] = acc[:, A_Q_W + 2 * A_KV_W:A_Q_W + 2 * A_KV_W + POOL_WIDTH]


def _even_inproj(ctx2, x2, modt, w_ext, cos2, sin2, gvec, ones_bd, *, batch, nt):
    nbpb = nt // TM
    nblk = batch * nbpb
    n_ext = w_ext.shape[1]
    kern = functools.partial(_even_inproj_kernel, nbpb=nbpb)
    return pl.pallas_call(
        kern,
        grid=(nblk,),
        in_specs=[*_token_specs(nbpb),
                  _mod_spec(0, 0, 1), _mod_spec(0, 1, 1),
                  pl.BlockSpec((D_MODEL, n_ext), lambda j: (0, 0)),
                  pl.BlockSpec((TM, LANES), lambda j: (j % nbpb, 0)),
                  pl.BlockSpec((TM, LANES), lambda j: (j % nbpb, 0)),
                  pl.BlockSpec((8, LANES), lambda j: (0, 0)),
                  pl.BlockSpec((LANES, LANES), lambda j: (0, 0))],
        out_specs=[pl.BlockSpec((None, A_Q_HEADS * LANES, TM), lambda j: (j // nbpb, 0, j % nbpb)),
                   pl.BlockSpec((TM, A_KV_W), lambda j: (j, 0)),
                   pl.BlockSpec((None, None, A_KV_HEADS * VROWS, TM), lambda j: (j // nbpb, j % nbpb, 0, 0)),
                   pl.BlockSpec((TM, POOL_WIDTH), lambda j: (j, 0))],
        out_shape=[jax.ShapeDtypeStruct((batch, A_Q_HEADS * LANES, nt), BF16),
                   jax.ShapeDtypeStruct((batch * nt, A_KV_W), BF16),
                   jax.ShapeDtypeStruct((batch, nbpb, A_KV_HEADS * VROWS, TM), BF16),
                   jax.ShapeDtypeStruct((batch * nt, POOL_WIDTH), F32)],
        compiler_params=_cparams(("arbitrary",), 40 << 20),
        name="even_inproj",
    )(ctx2, x2, modt, modt, w_ext, cos2, sin2, gvec, ones_bd)


def _flash_kernel(q_ref, k_ref, v_ref, o_ref, acc_ref, m_ref, out_ref, *, nq, k_off, v_off, ctx_queries, nsub):
    nch = v_ref.shape[0]
    acc_ref[...] = jnp.zeros_like(acc_ref)
    m_ref[...] = jnp.full_like(m_ref, NEG)

    def scores(j, c):
        start = c * TM if isinstance(c, int) else pl.multiple_of(c * TM, TM)
        kc = k_ref[pl.ds(start, TM), k_off[j]:k_off[j] + LANES]
        return _dot(kc, q_ref[j * LANES:(j + 1) * LANES, :])

    def update(j, c, s):
        row = slice(j, j + 1)
        m_old = m_ref[row, :]
        m_new = jnp.maximum(m_old, jnp.max(s, axis=0, keepdims=True))
        alpha = jnp.exp2(m_old - m_new)
        p = jnp.exp2(s - m_new)
        m_ref[row, :] = m_new
        pv = _dot(v_ref[c, v_off[j]:v_off[j] + VROWS, :], p.astype(BF16))
        rows = slice(j * VROWS, (j + 1) * VROWS)
        acc_ref[rows, :] = alpha * acc_ref[rows, :] + pv

    def run(chunks):
        stages = [(j, c) for c in chunks for j in range(nq)]
        ahead = [scores(*st) for st in stages[:FLASH_LOOKAHEAD]]
        for i, (j, c) in enumerate(stages):
            if i + FLASH_LOOKAHEAD < len(stages):
                ahead.append(scores(*stages[i + FLASH_LOOKAHEAD]))
            update(j, c, ahead.pop(0))

    def all_chunks():
        if nsub == nch:
            run(list(range(nch)))
        else:
            def body(it, carry):
                run([it * nsub + t for t in range(nsub)])
                return carry

            lax.fori_loop(0, nch // nsub, body, 0)

    if ctx_queries:
        pl.when(pl.program_id(2) == 0)(lambda: run([0]))
        pl.when(pl.program_id(2) != 0)(all_chunks)
    else:
        all_chunks()
    for j in range(nq):
        denom = acc_ref[j * VROWS + HEAD_DIM:j * VROWS + HEAD_DIM + 1, :]
        out_ref[j * HEAD_DIM:(j + 1) * HEAD_DIM, :] = acc_ref[j * VROWS:j * VROWS + HEAD_DIM, :] / denom
    o_ref[...] = out_ref[...].T.astype(BF16)


def _flash(qt, k, vt, *, batch, nt, units, nq, kw, k_off, v_off, ctx_queries, name):
    nbpb = nt // TM
    nqb = nbpb if ctx_queries else nbpb - 1
    qoff = 0 if ctx_queries else 1
    nv = vt.shape[2] // units
    nsub = max(d for d in range(1, FLASH_STAGES // nq + 1) if nbpb % d == 0)
    kern = functools.partial(_flash_kernel, nq=nq, k_off=k_off, v_off=v_off, ctx_queries=ctx_queries, nsub=nsub)
    vmem = 2 * (nt * kw * 2 + nt * nv * 2) + 4 * nq * LANES * TM * 2 + 3 * nq * HEAD_DIM * TM * 4 + (8 << 20)
    return pl.pallas_call(
        kern,
        grid=(batch, units, nqb),
        in_specs=[pl.BlockSpec((None, nq * LANES, TM), lambda b, u, i: (b, u, i + qoff)),
                  pl.BlockSpec((None, nt, kw), lambda b, u, i: (b, 0, u)),
                  pl.BlockSpec((None, nbpb, nv, TM), lambda b, u, i: (b, 0, u, 0))],
        out_specs=pl.BlockSpec((None, TM, nq * HEAD_DIM), lambda b, u, i: (b, i, u)),
        out_shape=jax.ShapeDtypeStruct((batch, nqb * TM, units * nq * HEAD_DIM), BF16),
        scratch_shapes=[pltpu.VMEM((nq * VROWS, TM), F32),
                        pltpu.VMEM((8, TM), F32),
                        pltpu.VMEM((nq * HEAD_DIM, TM), F32)],
        compiler_params=_cparams(("arbitrary", "arbitrary", "arbitrary"), vmem),
        name=name,
    )(qt, k, vt)


def _pool_kernel(prev_ref, cur_ref, next_ref, w_ref, ps_ref, o_ref, *, nbpb, nt):
    jj = pl.program_id(0) % nbpb
    base = jj * TM
    seg_lo = jnp.where(jj == 0, 0, TM)
    seg_hi = jnp.where(jj == 0, TM, nt)
    cur = cur_ref[...]
    xcat = jnp.concatenate([prev_ref[...], cur, next_ref[...]], axis=0)
    x_hi = xcat.astype(BF16)
    x_lo = (xcat - x_hi.astype(F32)).astype(BF16)
    t = base + lax.broadcasted_iota(jnp.int32, (TM, 3 * TM), 0)
    s = base - TM + lax.broadcasted_iota(jnp.int32, (TM, 3 * TM), 1)
    t1 = base + lax.broadcasted_iota(jnp.int32, (TM, 1), 0)
    lane_grp = lax.broadcasted_iota(jnp.int32, (TM, POOL_WIDTH), 1) // POOL_GROUP
    mean = jnp.zeros((TM, POOL_WIDTH), F32)
    for g, win in enumerate(POOL_WINDOWS):
        lo = win // 2
        hi = win - lo
        valid = ((s >= jnp.maximum(t - lo, seg_lo)) & (s < jnp.minimum(t + hi, seg_hi)))
        vb = valid.astype(F32).astype(BF16)
        wsum = _dot(vb, x_hi) + _dot(vb, x_lo)
        cnt = (jnp.minimum(t1 + hi, seg_hi) - jnp.maximum(t1 - lo, seg_lo)).astype(F32)
        mean = jnp.where(lane_grp == g, wsum / cnt, mean)
    y = (mean - cur).astype(BF16)
    o_ref[...] = (_dot(y, w_ref[...]) * ps_ref[...]).astype(BF16)


def _pool(p, w_bd, ps, *, batch, nt):
    nbpb = nt // TM
    nblk = batch * nbpb
    kern = functools.partial(_pool_kernel, nbpb=nbpb, nt=nt)

    def prev_map(j):
        return (jnp.maximum(j - 1, 0), 0)

    def next_map(j):
        return (jnp.minimum(j + 1, nblk - 1), 0)

    return pl.pallas_call(
        kern,
        grid=(nblk,),
        in_specs=[pl.BlockSpec((TM, POOL_WIDTH), prev_map),
                  pl.BlockSpec((TM, POOL_WIDTH), lambda j: (j, 0)),
                  pl.BlockSpec((TM, POOL_WIDTH), next_map),
                  pl.BlockSpec((POOL_WIDTH, POOL_WIDTH), lambda j: (0, 0)),
                  pl.BlockSpec((1, POOL_WIDTH), lambda j: (0, 0))],
        out_specs=pl.BlockSpec((TM, POOL_WIDTH), lambda j: (j, 0)),
        out_shape=jax.ShapeDtypeStruct((batch * nt, POOL_WIDTH), BF16),
        compiler_params=_cparams(("arbitrary",), 24 << 20),
        name="pool_mixer",
    )(p, p, p, w_bd, ps)


def _token_specs(nbpb):
    spb = nbpb - 1
    return [pl.BlockSpec((TM, D_MODEL), lambda j: (j // nbpb, 0)),
            pl.BlockSpec((TM, D_MODEL), lambda j: ((j // nbpb) * spb + jnp.maximum(j % nbpb - 1, 0), 0))]


def _token_block(c_ref, x_ref, nbpb):
    return jnp.where(pl.program_id(0) % nbpb == 0, c_ref[...], x_ref[...])


def _proj_ln_kernel(a_ref, b_ref, wa_ref, wb_ref, *rest, row_fn, nbpb):
    if nbpb:
        c_ref, x_ref, gate_ref, g_ref, beta_ref, o_ref = rest
        x = _token_block(c_ref, x_ref, nbpb)
    else:
        x_ref, gate_ref, g_ref, beta_ref, o_ref = rest
        x = x_ref[...]
    r = row_fn(pl.program_id(0))
    gate = gate_ref[pl.ds(r, 1), :]
    y = _dot(a_ref[...], wa_ref[...]) + _dot(b_ref[...], wb_ref[...])
    o_ref[...] = _layer_norm(ALPHA * x + gate * y, g_ref[...], beta_ref[...])


def _proj_ln(a, b, wa, wb, xs, modt, g, beta, *, layer, nblk, x_specs, row_fn, name, nbpb=0):
    ka, kb = a.shape[1], b.shape[1]
    kern = functools.partial(_proj_ln_kernel, row_fn=row_fn, nbpb=nbpb)
    return pl.pallas_call(
        kern,
        grid=(nblk,),
        in_specs=[pl.BlockSpec((TM, ka), lambda j: (j, 0)),
                  pl.BlockSpec((TM, kb), lambda j: (j, 0)),
                  pl.BlockSpec((ka, D_MODEL), lambda j: (0, 0)),
                  pl.BlockSpec((kb, D_MODEL), lambda j: (0, 0)),
                  *x_specs,
                  _mod_spec(layer, 2, 1),
                  pl.BlockSpec((1, D_MODEL), lambda j: (0, 0)),
                  pl.BlockSpec((1, D_MODEL), lambda j: (0, 0))],
        out_specs=pl.BlockSpec((TM, D_MODEL), lambda j: (j, 0)),
        out_shape=jax.ShapeDtypeStruct((nblk * TM, D_MODEL), F32),
        compiler_params=_cparams(("arbitrary",), 32 << 20),
        name=name,
    )(a, b, wa, wb, *xs, modt, g, beta)


def _ffn_ln_kernel(x_ref, sh_ref, sc_ref, gate_ref, wg_ref, wu_ref, wd_ref, g_ref, beta_ref, o_ref, *, nbpb):
    r = _mod_row_all(pl.program_id(0), nbpb)
    x = x_ref[...]
    u = (x * (1.0 + sc_ref[pl.ds(r, 1), :]) + sh_ref[pl.ds(r, 1), :]).astype(BF16)
    hg = _dot(u, wg_ref[...])
    hu = _dot(u, wu_ref[...])
    h = (hg * jax.nn.sigmoid(hg) * hu).astype(BF16)
    y = _dot(h, wd_ref[...])
    gate = gate_ref[pl.ds(r, 1), :]
    o_ref[...] = _layer_norm(ALPHA * x + gate * y, g_ref[...], beta_ref[...])


def _resident(shape):
    return pl.BlockSpec(shape, lambda *_: (0,) * len(shape), pipeline_mode=pl.Buffered(1))


def _ffn_ln(x1, modt, wg, wu, wd, g, beta, *, nbpb, nblk):
    dff = wg.shape[1]
    kern = functools.partial(_ffn_ln_kernel, nbpb=nbpb)
    return pl.pallas_call(
        kern,
        grid=(nblk,),
        in_specs=[pl.BlockSpec((TM, D_MODEL), lambda j: (j, 0)),
                  _mod_spec(0, 3, 1), _mod_spec(0, 4, 1), _mod_spec(0, 5, 1),
                  _resident((D_MODEL, dff)), _resident((D_MODEL, dff)), _resident((dff, D_MODEL)),
                  pl.BlockSpec((1, D_MODEL), lambda j: (0, 0)),
                  pl.BlockSpec((1, D_MODEL), lambda j: (0, 0))],
        out_specs=pl.BlockSpec((TM, D_MODEL), lambda j: (j, 0)),
        out_shape=jax.ShapeDtypeStruct((nblk * TM, D_MODEL), F32),
        compiler_params=_cparams(("arbitrary",), 44 << 20),
        name="even_ffn_ln",
    )(x1, modt, modt, modt, wg, wu, wd, g, beta)


def _odd_inproj_kernel(x_ref, sh_ref, sc_ref, w_ref, gq_ref, gkv_ref, wq_ref, wqp_ref, wk_ref, wv_ref,
                       pl2_ref, cq_ref, sq_ref, cs_ref,
                       qt_ref, k_ref, vt_ref, nq_ref, nk_ref, nv_ref, *, nbpb):
    j = pl.program_id(0)
    r = _mod_row_all(j, nbpb)
    u = x_ref[...] * (1.0 + sc_ref[pl.ds(r, 1), :]) + sh_ref[pl.ds(r, 1), :]
    acc = _dot(u.astype(BF16), w_ref[...])
    o = 0
    cq = acc[:, o:o + C_Q_RANK]; o += C_Q_RANK
    ckv = acc[:, o:o + C_KV_RANK]; o += C_KV_RANK
    nq = acc[:, o:o + NA_W]; o += NA_W
    nk = acc[:, o:o + NA_W]; o += NA_W
    nv = acc[:, o:o + NA_W]; o += NA_W
    krc = acc[:, o:o + LANES]

    def rms(a, g):
        return (a * lax.rsqrt(jnp.mean(a * a, axis=-1, keepdims=True) + NORM_EPS) * g).astype(BF16)

    cqn = rms(cq, gq_ref[...])
    kvn = rms(ckv, gkv_ref[...])
    cos_q = jnp.tile(cq_ref[...], (1, C_HEADS))
    sin_q = jnp.tile(sq_ref[...], (1, C_HEADS))
    q = _dot(cqn, wq_ref[...]) * cos_q + _dot(cqn, wqp_ref[...]) * sin_q
    q = q * ((C_NOPE + C_ROPE) ** -0.5 * LOG2E)
    qt_ref[...] = q.T.astype(BF16)
    kr_terms = (krc * cs_ref[...]).astype(BF16)
    k_ref[...] = (_dot(kvn, wk_ref[...]) + _dot(kr_terms, pl2_ref[...])).astype(BF16)
    _store_vt(vt_ref, _dot(kvn, wv_ref[...]).T.astype(BF16), C_HEADS)
    nq_ref[...] = (nq * HEAD_DIM ** -0.5).astype(BF16)
    nk_ref[...] = nk.astype(BF16)
    nv_ref[...] = nv.astype(BF16)


def _odd_inproj(xa, modt, w_ext, gq, gkv, wq_pad, wq_padp, wk_pad, wv, pl2, cos_q, sin_q, cs_k, *, batch, nt):
    nbpb = nt // TM
    nblk = batch * nbpb
    n_ext = w_ext.shape[1]
    hq = C_HEADS * LANES
    hv = C_HEADS * C_V
    kern = functools.partial(_odd_inproj_kernel, nbpb=nbpb)
    const = lambda j: (0, 0)
    tok = lambda j: (j, 0)
    pos = lambda j: (j % nbpb, 0)
    return pl.pallas_call(
        kern,
        grid=(nblk,),
        in_specs=[pl.BlockSpec((TM, D_MODEL), tok),
                  _mod_spec(1, 0, 1), _mod_spec(1, 1, 1),
                  pl.BlockSpec((D_MODEL, n_ext), const),
                  pl.BlockSpec((1, C_Q_RANK), const),
                  pl.BlockSpec((1, C_KV_RANK), const),
                  pl.BlockSpec((C_Q_RANK, hq), const),
                  pl.BlockSpec((C_Q_RANK, hq), const),
                  pl.BlockSpec((C_KV_RANK, hq), const),
                  pl.BlockSpec((C_KV_RANK, hv), const),
                  pl.BlockSpec((LANES, hq), const),
                  pl.BlockSpec((TM, LANES), pos),
                  pl.BlockSpec((TM, LANES), pos),
                  pl.BlockSpec((TM, LANES), pos)],
        out_specs=[pl.BlockSpec((None, hq, TM), lambda j: (j // nbpb, 0, j % nbpb)),
                   pl.BlockSpec((TM, hq), tok),
                   pl.BlockSpec((None, None, C_HEADS * VROWS, TM), lambda j: (j // nbpb, j % nbpb, 0, 0)),
                   pl.BlockSpec((TM, NA_W), tok),
                   pl.BlockSpec((TM, NA_W), tok),
                   pl.BlockSpec((TM, NA_W), tok)],
        out_shape=[jax.ShapeDtypeStruct((batch, hq, nt), BF16),
                   jax.ShapeDtypeStruct((batch * nt, hq), BF16),
                   jax.ShapeDtypeStruct((batch, nbpb, C_HEADS * VROWS, TM), BF16),
                   jax.ShapeDtypeStruct((batch * nt, NA_W), BF16),
                   jax.ShapeDtypeStruct((batch * nt, NA_W), BF16),
                   jax.ShapeDtypeStruct((batch * nt, NA_W), BF16)],
        compiler_params=_cparams(("arbitrary",), 48 << 20),
        name="odd_inproj",
    )(xa, modt, modt, w_ext, gq, gkv, wq_pad, wq_padp, wk_pad, wv, pl2, cos_q, sin_q, cs_k)


def _na_kernel(q_ref, kb_ref, vb_ref, kc_ref, vc_ref, bias_ref, o_ref):
    lane = lax.broadcasted_iota(jnp.int32, (GRID_W, LANES), 1)
    first_half = (lane // HEAD_DIM) == 0
    scores = []
    for h in range(D_HEADS):
        cols = slice((h // 2) * LANES, (h // 2 + 1) * LANES)
        qp = q_ref[:, cols]
        qh = jnp.where(first_half == (h % 2 == 0), qp, jnp.zeros_like(qp))
        s_w = _dot_nt(qh, kb_ref[0, :, cols]) + bias_ref[h]
        s_c = _dot_nt(qh, kc_ref[:, cols])
        scores.append((s_w, s_c))
    outs = []
    for h in range(D_HEADS):
        cols = slice((h // 2) * LANES, (h // 2 + 1) * LANES)
        s_w, s_c = scores[h]
        m = jnp.maximum(jnp.max(s_w, axis=-1, keepdims=True), jnp.max(s_c, axis=-1, keepdims=True))
        p_w = jnp.exp(s_w - m)
        p_c = jnp.exp(s_c - m)
        l = jnp.sum(p_w, axis=-1, keepdims=True) + jnp.sum(p_c, axis=-1, keepdims=True)
        o = _dot(p_w.astype(BF16), vb_ref[0, :, cols]) + _dot(p_c.astype(BF16), vc_ref[:, cols])
        outs.append(o / l)
    for hp in range(D_HEADS // 2):
        o_ref[:, hp * LANES:(hp + 1) * LANES] = jnp.where(first_half, outs[2 * hp], outs[2 * hp + 1]).astype(BF16)


def _na(nq, nk, nv, bias_tab, *, batch, nt, seq):
    rows = seq // GRID_W
    kh = NA_WIN_H
    band = kh * GRID_W
    ctx_rows = TM // GRID_W

    def rs_of(r):
        return jnp.clip(r - kh // 2, 0, rows - kh)

    nq3 = nq.reshape(batch, nt, NA_W)
    nk3 = nk.reshape(batch, nt, NA_W)
    nv3 = nv.reshape(batch, nt, NA_W)
    band_spec = pl.BlockSpec((pl.Element(1), pl.Element(band), pl.Element(NA_W)),
                             lambda b, r: (b, pl.multiple_of(TM + rs_of(r) * GRID_W, GRID_W), 0))
    ctx_spec = pl.BlockSpec((None, TM, NA_W), lambda b, r: (b, 0, 0))
    return pl.pallas_call(
        _na_kernel,
        grid=(batch, rows),
        in_specs=[pl.BlockSpec((None, GRID_W, NA_W), lambda b, r: (b, ctx_rows + r, 0)),
                  band_spec, band_spec, ctx_spec, ctx_spec,
                  pl.BlockSpec((None, D_HEADS, GRID_W, band), lambda b, r: (rs_of(r) - r + NA_WIN_H - 1, 0, 0, 0))],
        out_specs=pl.BlockSpec((None, GRID_W, NA_W), lambda b, r: (b, r, 0)),
        out_shape=jax.ShapeDtypeStruct((batch, seq, NA_W), BF16),
        compiler_params=_cparams(("arbitrary", "arbitrary"), 24 << 20),
        name="neighborhood_attn",
    )(nq3, nk3, nv3, nk3, nv3, bias_tab)


def _na_bias_table(rel_bias):
    j = np.arange(GRID_W)
    col_start = np.clip(j - NA_WIN_W // 2, 0, GRID_W - NA_WIN_W)
    col_valid = (j[None, :] >= col_start[:, None]) & (j[None, :] < col_start[:, None] + NA_WIN_W)
    col_idx = np.clip(j[None, :] - j[:, None] + NA_WIN_W - 1, 0, 2 * NA_WIN_W - 2)
    ds = np.arange(NA_WIN_H)[:, None] + np.arange(NA_WIN_H)[None, :]
    onehot = jnp.asarray(col_idx[:, :, None] == np.arange(2 * NA_WIN_W - 1), F32)
    tab = jnp.einsum('hdic,qkc->hdiqk', rel_bias[:, ds], onehot,
                     precision=lax.Precision.HIGHEST)
    tab = jnp.where(col_valid[None, None, None], tab, NEG)
    tab = tab.transpose(1, 0, 3, 2, 4)
    return tab.reshape(NA_WIN_H, D_HEADS, GRID_W, NA_WIN_H * GRID_W).astype(F32)


def _router_kernel(x_ref, sh_ref, sc_ref, wh_ref, wl_ref, b_ref, tri_ref, o_ref, cnt_ref, u_ref, base_ref, *, spb):
    j = pl.program_id(0)

    @pl.when(j == 0)
    def _():
        base_ref[...] = jnp.zeros_like(base_ref)

    r = 1 + j // spb
    u = x_ref[...] * (1.0 + sc_ref[pl.ds(r, 1), :]) + sh_ref[pl.ds(r, 1), :]
    u_ref[...] = u
    u_hi = u.astype(BF16)
    u_lo = (u - u_hi.astype(F32)).astype(BF16)
    wh = wh_ref[...]
    logits = _dot(u_hi, wh) + _dot(u_lo, wh) + _dot(u_hi, wl_ref[...]) + b_ref[...]
    lane = lax.broadcasted_iota(jnp.int32, (TM, LANES), 1)
    m1 = jnp.max(logits, axis=-1, keepdims=True)
    i1 = jnp.min(jnp.where(logits == m1, lane, LANES), axis=-1, keepdims=True)
    rest = jnp.where(lane == i1, NEG * 2, logits)
    m2 = jnp.max(rest, axis=-1, keepdims=True)
    i2 = jnp.min(jnp.where(rest == m2, lane, LANES), axis=-1, keepdims=True)
    e21 = jnp.exp(m2 - m1)
    g1 = 1.0 / (1.0 + e21)
    g2 = e21 / (1.0 + e21)
    oh1 = (lane == i1).astype(F32)
    oh2 = (lane == i2).astype(F32)
    both = oh1 + oh2
    before = _dot(tri_ref[...], both.astype(BF16)) + base_ref[0:1, :]
    r1 = jnp.sum(oh1 * before, axis=-1, keepdims=True)
    r2 = jnp.sum(oh2 * before, axis=-1, keepdims=True)
    out = jnp.zeros((TM, LANES), F32)
    for idx, col in enumerate((i1.astype(F32), i2.astype(F32), g1, g2, r1, r2)):
        out = jnp.where(lane == idx, col, out)
    o_ref[...] = out
    total = base_ref[0:1, :] + jnp.sum(both, axis=0, keepdims=True)
    base_ref[...] = jnp.broadcast_to(total, base_ref.shape)
    cnt_ref[...] = jnp.broadcast_to(total, cnt_ref.shape)


def _router(x1, modt, wr_hi, wr_lo, br, tri, *, nblk, spb):
    kern = functools.partial(_router_kernel, spb=spb)
    return pl.pallas_call(
        kern,
        grid=(nblk,),
        in_specs=[pl.BlockSpec((TM, D_MODEL), lambda j: (j, 0)),
                  _mod_spec(1, 3, 1), _mod_spec(1, 4, 1),
                  pl.BlockSpec((D_MODEL, LANES), lambda j: (0, 0)),
                  pl.BlockSpec((D_MODEL, LANES), lambda j: (0, 0)),
                  pl.BlockSpec((1, LANES), lambda j: (0, 0)),
                  pl.BlockSpec((TM, TM), lambda j: (0, 0))],
        out_specs=[pl.BlockSpec((TM, LANES), lambda j: (j, 0)),
                   pl.BlockSpec((8, LANES), lambda j: (0, 0)),
                   pl.BlockSpec((TM, D_MODEL), lambda j: (j, 0))],
        out_shape=[jax.ShapeDtypeStruct((nblk * TM, LANES), F32),
                   jax.ShapeDtypeStruct((8, LANES), F32),
                   jax.ShapeDtypeStruct((nblk * TM, D_MODEL), F32)],
        scratch_shapes=[pltpu.VMEM((8, LANES), F32)],
        compiler_params=_cparams(("arbitrary",), 24 << 20),
        name="moe_router",
    )(x1, modt, modt, wr_hi, wr_lo, br, tri)


def _expert_kernel(be_ref, src0_ref, srcn_ref, dstp_ref, u_ref, wg_ref, wu_ref, wd_ref, y_ref,
                   xbuf_ref, xb_ref, obuf_ref, gsem, ssem):
    del be_ref
    i = pl.program_id(0)
    f = pl.program_id(1)
    last_i = pl.num_programs(0) - 1
    slot = i % 2
    other = 1 - slot
    half = MOE_BM // MOE_NF

    def gather_row(src_ref, r, dst_slot):
        return pltpu.make_async_copy(u_ref.at[pl.ds(src_ref[0, r], 1)],
                                     xbuf_ref.at[dst_slot, pl.ds(r, 1)], gsem.at[dst_slot])

    def scatter_row(r, src_slot):
        return pltpu.make_async_copy(obuf_ref.at[src_slot, pl.ds(r, 1)],
                                     y_ref.at[pl.ds(dstp_ref[0, r], 1)], ssem.at[src_slot])

    def gather_block_wait(s):
        pltpu.make_async_copy(u_ref.at[pl.ds(0, MOE_BM)], xbuf_ref.at[s], gsem.at[s]).wait()

    def scatter_block_wait(s):
        pltpu.make_async_copy(obuf_ref.at[s], y_ref.at[pl.ds(0, MOE_BM)], ssem.at[s]).wait()

    @pl.when((i == 0) & (f == 0))
    def _():
        obuf_ref[...] = jnp.zeros_like(obuf_ref)

        def first(r, c):
            gather_row(src0_ref, r, 0).start()
            return c

        lax.fori_loop(0, MOE_BM, first, 0)

    @pl.when(f == 0)
    def _():
        gather_block_wait(slot)
        xb_ref[...] = xbuf_ref[slot].astype(BF16)

    @pl.when((f == 0) & (i >= 1))
    def _():
        scatter_block_wait(slot)

    for t in range(half):
        r = f * half + t
        gather_row(srcn_ref, r, other).start(priority=0)
        scatter_row(r, other).start(priority=1)

    xb = xb_ref[...]
    hg = _dot(xb, wg_ref[...])
    hu = _dot(xb, wu_ref[...])
    h = (hg * jax.nn.sigmoid(hg) * hu).astype(BF16)
    part = _dot(h, wd_ref[...])

    @pl.when(f == 0)
    def _():
        obuf_ref[slot] = part

    @pl.when(f > 0)
    def _():
        obuf_ref[slot] += part

    @pl.when((i == last_i) & (f == MOE_NF - 1))
    def _():
        gather_block_wait(other)
        scatter_block_wait(other)


def _experts(blk_expert, src_ext, dst_ext, u, wg, wu, wd, *, n_out):
    nb = src_ext.shape[0] - 2
    assert wg.shape[2] == MOE_NF * MOE_TF and dst_ext.shape[0] == nb + 1
    idx = lambda fn: pl.BlockSpec((None, 1, MOE_BM), fn, memory_space=pltpu.SMEM)
    grid_spec = pltpu.PrefetchScalarGridSpec(
        num_scalar_prefetch=1,
        grid=(nb + 1, MOE_NF),
        in_specs=[idx(lambda i, f, be: (0, 0, 0)),
                  idx(lambda i, f, be: (i + 1, 0, 0)),
                  idx(lambda i, f, be: (i, 0, 0)),
                  pl.BlockSpec(memory_space=pl.ANY),
                  pl.BlockSpec((None, D_MODEL, MOE_TF), lambda i, f, be: (be[i], 0, f)),
                  pl.BlockSpec((None, D_MODEL, MOE_TF), lambda i, f, be: (be[i], 0, f)),
                  pl.BlockSpec((None, MOE_TF, D_MODEL), lambda i, f, be: (be[i], f, 0))],
        out_specs=pl.BlockSpec(memory_space=pl.ANY),
        scratch_shapes=[pltpu.VMEM((2, MOE_BM, D_MODEL), F32),
                        pltpu.VMEM((MOE_BM, D_MODEL), BF16),
                        pltpu.VMEM((2, MOE_BM, D_MODEL), F32),
                        pltpu.SemaphoreType.DMA((2,)),
                        pltpu.SemaphoreType.DMA((2,))],
    )
    return pl.pallas_call(
        _expert_kernel,
        grid_spec=grid_spec,
        out_shape=jax.ShapeDtypeStruct((n_out, D_MODEL), F32),
        compiler_params=_cparams(("arbitrary", "arbitrary"), 52 << 20),
        name="moe_experts",
    )(blk_expert, src_ext, src_ext, dst_ext, u, wg, wu, wd)


def _moe_tables(dest, t_tok, n_blocks):
    npad = n_blocks * MOE_BM
    pos = jnp.arange(npad, dtype=jnp.int32)
    trash = 2 * t_tok + ((pos // MOE_BM) % 2) * MOE_BM + pos % MOE_BM
    flat = dest.reshape(-1)
    asg = jnp.arange(2 * t_tok, dtype=jnp.int32)
    tok, slot = asg // 2, asg % 2
    init = jnp.stack([jnp.zeros((npad,), jnp.int32), trash], axis=1)
    both = init.at[flat].set(jnp.stack([tok, slot * t_tok + tok], axis=1),
                             unique_indices=True, mode='promise_in_bounds')
    src, dst = both[:, 0], both[:, 1]
    src_ext = jnp.concatenate([src, jnp.zeros((2 * MOE_BM,), jnp.int32)]).reshape(n_blocks + 2, 1, MOE_BM)
    lead = 2 * t_tok + MOE_BM + jnp.arange(MOE_BM, dtype=jnp.int32)
    dst_ext = jnp.concatenate([lead, dst]).reshape(n_blocks + 1, 1, MOE_BM)
    return src_ext, dst_ext


def _combine_kernel(y0_ref, y1_ref, x_ref, rt_ref, gate_ref, g_ref, beta_ref, o_ref, *, spb):
    r = 1 + pl.program_id(0) // spb
    rt = rt_ref[...]
    y = rt[:, 2:3] * y0_ref[...] + rt[:, 3:4] * y1_ref[...]
    gate = gate_ref[pl.ds(r, 1), :]
    o_ref[...] = _layer_norm(ALPHA * x_ref[...] + gate * y, g_ref[...], beta_ref[...])


def _combine(y2, x1, rt, modt, g, beta, *, nblk, spb):
    kern = functools.partial(_combine_kernel, spb=spb)
    return pl.pallas_call(
        kern,
        grid=(nblk,),
        in_specs=[pl.BlockSpec((TM, D_MODEL), lambda j: (j, 0)),
                  pl.BlockSpec((TM, D_MODEL), lambda j: (j + nblk, 0)),
                  pl.BlockSpec((TM, D_MODEL), lambda j: (j, 0)),
                  pl.BlockSpec((TM, LANES), lambda j: (j, 0)),
                  _mod_spec(1, 5, 1),
                  pl.BlockSpec((1, D_MODEL), lambda j: (0, 0)),
                  pl.BlockSpec((1, D_MODEL), lambda j: (0, 0))],
        out_specs=pl.BlockSpec((TM, D_MODEL), lambda j: (j, 0)),
        out_shape=jax.ShapeDtypeStruct((nblk * TM, D_MODEL), F32),
        compiler_params=_cparams(("arbitrary",), 24 << 20),
        name="moe_combine_ln",
    )(y2, y2, x1, rt, modt, g, beta)


def _head_perm(partner, heads, width):
    return np.concatenate([h * width + partner for h in range(heads)])


def _even_weights(w_in, q_gain, k_gain, partner):
    perm_q = _head_perm(partner, A_Q_HEADS, HEAD_DIM)
    perm_k = _head_perm(partner, A_KV_HEADS, HEAD_DIM)
    wq = w_in[:, :A_Q_W]
    wk = w_in[:, A_Q_W:A_Q_W + A_KV_W]
    w_ext = jnp.concatenate([w_in, wq[:, perm_q], wk[:, perm_k]], axis=1).astype(BF16)
    gvec = jnp.zeros((8, LANES), F32)
    gvec = gvec.at[0].set(jnp.tile(q_gain, 2)).at[1].set(jnp.tile(q_gain[partner], 2))
    gvec = gvec.at[2].set(jnp.tile(k_gain, 2)).at[3].set(jnp.tile(k_gain[partner], 2))
    return w_ext, gvec


def _block_diag(blocks):
    n = blocks.shape[0]
    w = blocks.shape[1]
    out = jnp.zeros((n * w, n * w), blocks.dtype)
    for g in range(n):
        out = out.at[g * w:(g + 1) * w, g * w:(g + 1) * w].set(blocks[g])
    return out


def _odd_weights(w_in, w_q_up, w_kv_up, partner):
    o_kr = C_Q_RANK + C_KV_RANK
    w_kr = w_in[:, o_kr:o_kr + C_ROPE]
    w_rest = jnp.concatenate([w_in[:, :o_kr], w_in[:, o_kr + C_ROPE:]], axis=1)
    kr_blk = jnp.concatenate([w_kr, w_kr[:, partner], jnp.zeros((D_MODEL, LANES - 2 * C_ROPE), F32)], axis=1)
    w_ext = jnp.concatenate([w_rest, kr_blk], axis=1).astype(BF16)
    dq = C_NOPE + C_ROPE
    wq3 = w_q_up.reshape(C_Q_RANK, C_HEADS, dq)
    zq = jnp.zeros((C_Q_RANK, C_HEADS, LANES - dq), F32)
    wq_pad = jnp.concatenate([wq3, zq], axis=-1).reshape(C_Q_RANK, C_HEADS * LANES)
    wq_rope_p = wq3[:, :, C_NOPE:][:, :, partner]
    wq_padp = jnp.concatenate([jnp.zeros((C_Q_RANK, C_HEADS, C_NOPE), F32), wq_rope_p, zq], axis=-1)
    wq_padp = wq_padp.reshape(C_Q_RANK, C_HEADS * LANES)
    wkv3 = w_kv_up.reshape(C_KV_RANK, C_HEADS, C_NOPE + C_V)
    wk_pad = jnp.concatenate([wkv3[:, :, :C_NOPE], jnp.zeros((C_KV_RANK, C_HEADS, LANES - C_NOPE), F32)], axis=-1)
    wk_pad = wk_pad.reshape(C_KV_RANK, C_HEADS * LANES)
    wv = wkv3[:, :, C_NOPE:].reshape(C_KV_RANK, C_HEADS * C_V)
    pl2 = np.zeros((LANES, C_HEADS * LANES), np.float32)
    for h in range(C_HEADS):
        for dd in range(C_ROPE):
            pl2[dd, h * LANES + C_NOPE + dd] = 1.0
            pl2[C_ROPE + dd, h * LANES + C_NOPE + dd] = 1.0
    return (w_ext, wq_pad.astype(BF16), wq_padp.astype(BF16), wk_pad.astype(BF16), wv.astype(BF16),
            jnp.asarray(pl2, BF16))


def kernel(x, c, ctx, c_ctx, w_ada, b_ada, ln1_g, ln1_b, ln2_g, ln2_b,
           ev_w_in, ev_w_out, ev_q_gain, ev_k_gain, ev_w_pool, ev_pool_scale,
           ev_w_gate, ev_w_up, ev_w_down,
           od_w_in, od_w_out, od_q_lat_gain, od_kv_lat_gain, od_w_q_up, od_w_kv_up, od_na_bias,
           od_w_router, od_b_router, od_w_gate, od_w_up, od_w_down):
    batch, seq, d = x.shape
    ctx_len = ctx.shape[1]
    assert d == D_MODEL and ctx_len == TM and seq % TM == 0 and seq % GRID_W == 0
    assert seq // GRID_W >= NA_WIN_H and batch + 1 <= 8
    nt = ctx_len + seq
    nbpb = nt // TM
    nblk = batch * nbpb
    spb = seq // TM
    nxb = batch * spb

    ctx2 = ctx.reshape(batch * ctx_len, d)
    x2 = x.reshape(batch * seq, d)
    cin = jnp.zeros((8, d), F32).at[0].set(c_ctx).at[1:1 + batch].set(c)
    modt = _ada_table(cin, w_ada, b_ada)

    cos64, sin64, partner64 = _rope_axis_tables(HEAD_DIM, seq, ctx_len)
    w_ext0, gvec = _even_weights(ev_w_in[0], ev_q_gain[0], ev_k_gain[0], partner64)
    cos2 = jnp.tile(cos64, (1, 2))
    sin2 = jnp.tile(sin64, (1, 2))
    ones_bd = _block_diag(jnp.ones((2, HEAD_DIM, HEAD_DIM), BF16))
    qt, kk, vt, pp = _even_inproj(ctx2, x2, modt, w_ext0, cos2, sin2, gvec, ones_bd, batch=batch, nt=nt)
    attn = _flash(qt, kk.reshape(batch, nt, A_KV_W), vt, batch=batch, nt=nt, units=A_KV_HEADS // 2,
                  nq=2 * A_GROUP, kw=LANES, k_off=(0,) * (2 * A_GROUP),
                  v_off=tuple((jq // A_GROUP) * VROWS for jq in range(2 * A_GROUP)),
                  ctx_queries=True, name="gqa_flash")
    pooled = _pool(pp, _block_diag(ev_w_pool[0]).astype(BF16), ev_pool_scale[0].reshape(1, POOL_WIDTH),
                   batch=batch, nt=nt)
    w_out0 = ev_w_out[0].astype(BF16)
    x1 = _proj_ln(attn.reshape(batch * nt, A_Q_W), pooled, w_out0[:A_Q_W], w_out0[A_Q_W:], (ctx2, x2), modt,
                  ln1_g[0].reshape(1, d), ln1_b[0].reshape(1, d), layer=0, nblk=nblk,
                  x_specs=_token_specs(nbpb), row_fn=lambda j: _mod_row_all(j, nbpb), name="even_out_ln",
                  nbpb=nbpb)
    xa1 = _ffn_ln(x1, modt, ev_w_gate[0].astype(BF16), ev_w_up[0].astype(BF16), ev_w_down[0].astype(BF16),
                  ln2_g[0].reshape(1, d), ln2_b[0].reshape(1, d), nbpb=nbpb, nblk=nblk)

    cos32, sin32, partner32 = _rope_axis_tables(C_ROPE, seq, ctx_len)
    w_ext1, wq_pad, wq_padp, wk_pad, wv, pl2 = _odd_weights(od_w_in[0], od_w_q_up[0], od_w_kv_up[0], partner32)
    ntr = cos32.shape[0]
    pad_tail = jnp.zeros((ntr, LANES - C_NOPE - C_ROPE), F32)
    cos_q = jnp.concatenate([jnp.ones((ntr, C_NOPE), F32), cos32, pad_tail], axis=1)
    sin_q = jnp.concatenate([jnp.zeros((ntr, C_NOPE), F32), sin32, pad_tail], axis=1)
    cs_k = jnp.concatenate([cos32, sin32, jnp.zeros((ntr, LANES - 2 * C_ROPE), F32)], axis=1)
    qt1, k1, vt1, nq, nk, nv = _odd_inproj(
        xa1, modt, w_ext1, od_q_lat_gain[0].reshape(1, C_Q_RANK), od_kv_lat_gain[0].reshape(1, C_KV_RANK),
        wq_pad, wq_padp, wk_pad, wv, pl2, cos_q, sin_q, cs_k, batch=batch, nt=nt)
    mla = _flash(qt1, k1.reshape(batch, nt, C_HEADS * LANES), vt1, batch=batch, nt=nt, units=C_HEADS // 2,
                 nq=2, kw=2 * LANES, k_off=(0, LANES), v_off=(0, VROWS), ctx_queries=False, name="mla_flash")
    na = _na(nq, nk, nv, _na_bias_table(od_na_bias[0]), batch=batch, nt=nt, seq=seq)
    w_out1 = od_w_out[0].astype(BF16)
    x1b = _proj_ln(mla.reshape(batch * seq, C_HEADS * C_V), na.reshape(batch * seq, NA_W),
                   w_out1[:C_HEADS * C_V], w_out1[C_HEADS * C_V:], (xa1,), modt,
                   ln1_g[1].reshape(1, d), ln1_b[1].reshape(1, d), layer=1, nblk=nxb,
                   x_specs=[pl.BlockSpec((TM, D_MODEL), lambda j: (j + j // spb + 1, 0))],
                   row_fn=lambda j: 1 + j // spb, name="odd_out_ln")

    wr = jnp.zeros((d, LANES), F32).at[:, :N_EXPERTS].set(od_w_router[0])
    wr_hi = wr.astype(BF16)
    wr_lo = (wr - wr_hi.astype(F32)).astype(BF16)
    br = jnp.full((1, LANES), NEG, F32).at[0, :N_EXPERTS].set(od_b_router[0])
    tri = jnp.asarray(np.tril(np.ones((TM, TM), np.float32), -1), BF16)
    rt, cnt, u_moe = _router(x1b, modt, wr_hi, wr_lo, br, tri, nblk=nxb, spb=spb)

    t_tok = batch * seq
    counts = cnt[0, :N_EXPERTS].astype(jnp.int32)
    padded = (counts + MOE_BM - 1) // MOE_BM * MOE_BM
    pad_end = jnp.cumsum(padded)
    pad_start = pad_end - padded
    n_blocks = -(-2 * t_tok // MOE_BM) + N_EXPERTS
    e12 = rt[:, 0:2].astype(jnp.int32)
    r12 = rt[:, 4:6].astype(jnp.int32)
    dest = pad_start[e12] + r12
    blk_start = jnp.arange(n_blocks + 1, dtype=jnp.int32) * MOE_BM
    blk_expert = jnp.minimum(jnp.sum((pad_end[None, :] <= blk_start[:, None]).astype(jnp.int32), axis=1),
                             N_EXPERTS - 1)
    src_ext, dst_ext = _moe_tables(dest, t_tok, n_blocks)

    y2 = _experts(blk_expert, src_ext, dst_ext, u_moe, od_w_gate[0].astype(BF16), od_w_up[0].astype(BF16),
                  od_w_down[0].astype(BF16), n_out=2 * t_tok + 2 * MOE_BM)
    out = _combine(y2, x1b, rt, modt, ln2_g[1].reshape(1, d), ln2_b[1].reshape(1, d), nblk=nxb, spb=spb)
    return out.reshape(batch, seq, d)
```
